```python
import math
import jax, jax.numpy as jnp
from jax import lax
import numpy as np

D_MODEL = 2048
BATCH = 2
SEQ = 16384
DEPTH = 1

GDN_HEADS = 8
GDN_DK = 128
GDN_DV = 128
GDN_CONV = 4
GDN_CHUNK = 64
GDN_KEY_W = GDN_HEADS * GDN_DK
GDN_VAL_W = GDN_HEADS * GDN_DV
GDN_QKV_W = 2 * GDN_KEY_W + GDN_VAL_W
MLA_HEADS = 8
MLA_Q_RANK = 512
MLA_KV_RANK = 256
MLA_NOPE = 128
MLA_ROPE = 64
MLA_V = 128
MLA_QK = MLA_NOPE + MLA_ROPE
MLA_VAL_W = MLA_HEADS * MLA_V
ROPE_THETA = 10000.0
Q_BLOCK = 128
NORM_EPS = 1e-6
POS_OFFSET_MAX = 1024

IN_WIDTHS = (
    GDN_QKV_W,
    GDN_VAL_W,
    GDN_HEADS,
    GDN_HEADS,
    MLA_Q_RANK,
    MLA_KV_RANK,
    MLA_ROPE,
    MLA_VAL_W,
    D_MODEL,
    D_MODEL,
)
D_IN = GDN_QKV_W + GDN_VAL_W + 2 * GDN_HEADS + MLA_Q_RANK + MLA_KV_RANK + MLA_ROPE + MLA_VAL_W + 2 * D_MODEL

kernel_name = "gated_deltanet_mla_parallel_hybrid"


def rms_norm(x, w):
    xf = x.astype(jnp.float32)
    y = xf * lax.rsqrt(jnp.mean(xf * xf, axis=-1, keepdims=True) + NORM_EPS)
    return (y * w.astype(jnp.float32)).astype(x.dtype)


def l2_norm(x):
    xf = x.astype(jnp.float32)
    return (xf * lax.rsqrt(jnp.sum(xf * xf, axis=-1, keepdims=True) + NORM_EPS)).astype(x.dtype)


def split_columns(t, widths):
    points, acc = [], 0
    for w in widths[:-1]:
        acc += w
        points.append(acc)
    return jnp.split(t, points, axis=-1)


def causal_short_conv(x, w):
    k_taps, s = w.shape[0], x.shape[1]
    xp = jnp.pad(x, ((0, 0), (k_taps - 1, 0), (0, 0)))
    out = xp[:, 0:s] * w[0]
    for i in range(1, k_taps):
        out = out + xp[:, i:i + s] * w[i]
    return out


def rope_tables(positions):
    inv_freq = ROPE_THETA ** (-jnp.arange(0, MLA_ROPE, 2, dtype=jnp.float32) / MLA_ROPE)
    ang = positions.astype(jnp.float32)[..., None] * inv_freq
    return jnp.cos(ang), jnp.sin(ang)


def apply_rope(x, cos, sin):
    x1, x2 = jnp.split(x, 2, axis=-1)
    cos, sin = cos.astype(x.dtype), sin.astype(x.dtype)
    return jnp.concatenate([x1 * cos - x2 * sin, x1 * sin + x2 * cos], axis=-1)


def gated_delta_rule_chunked(q, k, v, g, beta):
    out_dtype = v.dtype
    b, s, h, dk = q.shape
    dv = v.shape[-1]
    c = GDN_CHUNK
    n = s // c

    def to_chunks(t):
        t = t.astype(jnp.float32).reshape((b, n, c, h) + t.shape[3:])
        return jnp.moveaxis(t, 3, 1)

    q, k, v, g, beta = (to_chunks(t) for t in (q, k, v, g, beta))
    g = jnp.cumsum(g, axis=-1)
    idx = jnp.arange(c)
    causal = idx[:, None] >= idx[None, :]
    strict = idx[:, None] > idx[None, :]
    diff = g[..., :, None] - g[..., None, :]
    decay = jnp.where(causal, jnp.exp(jnp.where(causal, diff, 0.0)), 0.0)

    k_beta = k * beta[..., None]
    a_mat = jnp.where(strict, jnp.einsum('bhnid,bhnjd->bhnij', k_beta, k) * decay, 0.0)
    eye = jnp.eye(c, dtype=jnp.float32)
    t_mat = lax.linalg.triangular_solve(eye + a_mat, jnp.broadcast_to(eye, a_mat.shape),
                                        left_side=True, lower=True)
    u = jnp.einsum('bhnij,bhnjv->bhniv', t_mat, v * beta[..., None])
    w = jnp.einsum('bhnij,bhnjk->bhnik', t_mat, k_beta * jnp.exp(g)[..., None])
    attn = jnp.where(causal, jnp.einsum('bhnik,bhnjk->bhnij', q, k) * decay, 0.0)
    g_last = g[..., -1]
    q_dec = q * jnp.exp(g)[..., None]
    k_dec = k * jnp.exp(g_last[..., None] - g)[..., None]

    xs = tuple(jnp.moveaxis(t, 2, 0) for t in (q_dec, k_dec, u, w, attn, g_last))

    def step(state, inp):
        q_n, k_n, u_n, w_n, a_n, gl_n = inp
        v_new = u_n - jnp.einsum('bhck,bhkv->bhcv', w_n, state)
        o_n = jnp.einsum('bhck,bhkv->bhcv', q_n, state) + jnp.einsum('bhij,bhjv->bhiv', a_n, v_new)
        state = state * jnp.exp(gl_n)[..., None, None] + jnp.einsum('bhck,bhcv->bhkv', k_n, v_new)
        return state, o_n

    state0 = jnp.zeros((b, h, dk, dv), jnp.float32)
    _, o = lax.scan(step, state0, xs)
    o = jnp.transpose(o, (1, 0, 3, 2, 4)).reshape(b, s, h, dv)
    return o.astype(out_dtype)


def causal_block_attention(q, k, v, scale):
    b, s, h, dqk = q.shape
    dv = v.shape[-1]
    nb = s // Q_BLOCK
    q_blocks = jnp.moveaxis(q.reshape(b, nb, Q_BLOCK, h, dqk), 1, 0)
    key_pos = jnp.arange(s)

    def one_block(args):
        q_blk, blk = args
        sc = jnp.einsum('bqhd,bkhd->bhqk', q_blk, k).astype(jnp.float32) * scale
        q_pos = blk * Q_BLOCK + jnp.arange(Q_BLOCK)
        sc = jnp.where(key_pos[None, :] <= q_pos[:, None], sc, -jnp.inf)
        p = jax.nn.softmax(sc, axis=-1).astype(v.dtype)
        return jnp.einsum('bhqk,bkhd->bqhd', p, v)

    out = lax.map(one_block, (q_blocks, jnp.arange(nb)))
    return jnp.moveaxis(out, 0, 1).reshape(b, s, h, dv)


def hybrid_mixer(h, positions, w_in, conv_w, a_log, dt_bias, gdn_norm_w, q_norm_w, w_uq,
                 kv_norm_w, w_ukv, proj_a, proj_b, w_out):
    b, s, _ = h.shape
    proj = h @ w_in
    (qkv_a, z_a, alpha_a, beta_a, c_q, c_kv, k_rope, z_b, gate_a, gate_b) = split_columns(proj, IN_WIDTHS)

    qkv = jax.nn.silu(causal_short_conv(qkv_a, conv_w))
    q_a, k_a, v_a = jnp.split(qkv, [GDN_KEY_W, 2 * GDN_KEY_W], axis=-1)
    q_a = l2_norm(q_a.reshape(b, s, GDN_HEADS, GDN_DK)) * (GDN_DK ** -0.5)
    k_a = l2_norm(k_a.reshape(b, s, GDN_HEADS, GDN_DK))
    v_a = v_a.reshape(b, s, GDN_HEADS, GDN_DV)
    beta = jax.nn.sigmoid(beta_a)
    g_log = -jnp.exp(a_log) * jax.nn.softplus(alpha_a + dt_bias)
    o_a = gated_delta_rule_chunked(q_a, k_a, v_a, g_log, beta)
    o_a = rms_norm(o_a, gdn_norm_w) * jax.nn.silu(z_a.reshape(b, s, GDN_HEADS, GDN_DV))
    y_a = o_a.reshape(b, s, GDN_VAL_W) @ proj_a

    q_b = (rms_norm(c_q, q_norm_w) @ w_uq).reshape(b, s, MLA_HEADS, MLA_QK)
    q_nope, q_rope = jnp.split(q_b, [MLA_NOPE], axis=-1)
    kv = (rms_norm(c_kv, kv_norm_w) @ w_ukv).reshape(b, s, MLA_HEADS, MLA_NOPE + MLA_V)
    k_nope, v_b = jnp.split(kv, [MLA_NOPE], axis=-1)
    cos, sin = rope_tables(positions)
    q_rope = apply_rope(q_rope, cos[:, :, None], sin[:, :, None])
    k_rope = apply_rope(k_rope, cos, sin)
    q_full = jnp.concatenate([q_nope, q_rope], axis=-1)
    k_full = jnp.concatenate(
        [k_nope, jnp.broadcast_to(k_rope[:, :, None], (b, s, MLA_HEADS, MLA_ROPE))], axis=-1)
    o_b = causal_block_attention(q_full, k_full, v_b, 1.0 / math.sqrt(MLA_QK))
    o_b = o_b.reshape(b, s, MLA_VAL_W) * jax.nn.silu(z_b)
    y_b = o_b @ proj_b

    merged = jax.nn.sigmoid(gate_a) * y_a + jax.nn.sigmoid(gate_b) * y_b
    return merged @ w_out


def setup_inputs(seed: int = 0) -> dict:
    key = jax.random.key(seed)
    ks = jax.random.split(key, 18)
    f32 = jnp.float32
    x = jax.random.normal(ks[0], (BATCH, SEQ, D_MODEL), f32)
    offsets = jax.random.randint(ks[1], (BATCH, 1), 0, POS_OFFSET_MAX, dtype=jnp.int32)
    positions = (offsets + jnp.arange(SEQ, dtype=jnp.int32)[None, :]).astype(jnp.int32)
    norm_w = 1.0 + 0.02 * jax.random.normal(ks[2], (DEPTH, D_MODEL), f32)
    w_in = jax.random.normal(ks[3], (DEPTH, D_MODEL, D_IN), f32) * D_MODEL ** -0.5
    conv_w = jax.random.normal(ks[4], (DEPTH, GDN_CONV, GDN_QKV_W), f32) * GDN_CONV ** -0.5
    a_log = jnp.log(jax.random.uniform(ks[5], (DEPTH, GDN_HEADS), f32, 1.0, 16.0))
    dt = jnp.exp(jax.random.uniform(ks[6], (DEPTH, GDN_HEADS), f32, math.log(1e-3), math.log(1e-1)))
    dt_bias = dt + jnp.log(-jnp.expm1(-dt))
    gdn_norm_w = 1.0 + 0.02 * jax.random.normal(ks[7], (DEPTH, GDN_DV), f32)
    q_norm_w = 1.0 + 0.02 * jax.random.normal(ks[8], (DEPTH, MLA_Q_RANK), f32)
    w_uq = jax.random.normal(ks[9], (DEPTH, MLA_Q_RANK, MLA_HEADS * MLA_QK), f32) * MLA_Q_RANK ** -0.5
    kv_norm_w = 1.0 + 0.02 * jax.random.normal(ks[10], (DEPTH, MLA_KV_RANK), f32)
    w_ukv = jax.random.normal(ks[11], (DEPTH, MLA_KV_RANK, MLA_HEADS * (MLA_NOPE + MLA_V)), f32) * MLA_KV_RANK ** -0.5
    proj_a = jax.random.normal(ks[12], (DEPTH, GDN_VAL_W, D_MODEL), f32) * GDN_VAL_W ** -0.5
    proj_b = jax.random.normal(ks[13], (DEPTH, MLA_VAL_W, D_MODEL), f32) * MLA_VAL_W ** -0.5
    w_out = jax.random.normal(ks[14], (DEPTH, D_MODEL, D_MODEL), f32) * D_MODEL ** -0.5
    final_norm_w = 1.0 + 0.02 * jax.random.normal(ks[15], (D_MODEL,), f32)
    return {"x": x, "positions": positions, "norm_w": norm_w, "w_in": w_in, "conv_w": conv_w,
            "a_log": a_log, "dt_bias": dt_bias, "gdn_norm_w": gdn_norm_w, "q_norm_w": q_norm_w,
            "w_uq": w_uq, "kv_norm_w": kv_norm_w, "w_ukv": w_ukv, "proj_a": proj_a,
            "proj_b": proj_b, "w_out": w_out, "final_norm_w": final_norm_w}


def reference(x, positions, norm_w, w_in, conv_w, a_log, dt_bias, gdn_norm_w, q_norm_w, w_uq,
              kv_norm_w, w_ukv, proj_a, proj_b, w_out, final_norm_w):
    for layer in range(DEPTH):
        h = rms_norm(x, norm_w[layer])
        x = x + hybrid_mixer(h, positions, w_in[layer], conv_w[layer], a_log[layer], dt_bias[layer],
                             gdn_norm_w[layer], q_norm_w[layer], w_uq[layer], kv_norm_w[layer],
                             w_ukv[layer], proj_a[layer], proj_b[layer], w_out[layer])
    return rms_norm(x, final_norm_w)
```

```python
import functools
import math

import jax
import jax.numpy as jnp
from jax import lax
from jax.experimental import pallas as pl
from jax.experimental.pallas import tpu as pltpu

F32 = jnp.float32
BF16 = jnp.bfloat16
HIGHEST = lax.Precision.HIGHEST

LANES = 128
NORM_EPS = 1e-6
ROPE_THETA = 10000.0

GDN_HEADS = 8
GDN_DK = 128
GDN_DV = 128
GDN_CONV = 4
GDN_CHUNK = 64
MLA_HEADS = 8
MLA_NOPE = 128
MLA_ROPE = 64
MLA_V = 128
MLA_QK = MLA_NOPE + MLA_ROPE

VMEM_LIMIT = 56 * 1024 * 1024


def _cparams(sem):
    return pltpu.CompilerParams(dimension_semantics=sem, vmem_limit_bytes=VMEM_LIMIT)


def _silu(x):
    return x / (1.0 + jnp.exp(-x))


def _sigmoid(x):
    return 1.0 / (1.0 + jnp.exp(-x))


def _dot(a, b):
    return jnp.dot(a, b, preferred_element_type=F32)


def _dot_nt(a, b):
    return lax.dot_general(a, b, (((1,), (1,)), ((), ())), preferred_element_type=F32)


def _dot_tn(a, b):
    return lax.dot_general(a, b, (((0,), (0,)), ((), ())), preferred_element_type=F32)


def _dot_hi(a, b):
    return jnp.dot(a, b, preferred_element_type=F32, precision=HIGHEST)


def _inproj_kernel(x_ref, nw_ref, w_ref, o_ref, h_ref):
    @pl.when(pl.program_id(1) == 0)
    def _():
        x = x_ref[...]
        ms = jnp.mean(x * x, axis=-1, keepdims=True)
        h_ref[...] = (x * lax.rsqrt(ms + NORM_EPS) * nw_ref[...]).astype(BF16)

    o_ref[...] = _dot(h_ref[...], w_ref[...])


def _inproj(x2, norm_w, w_all, tm, tn):
    t, d = x2.shape
    n = w_all.shape[1]
    return pl.pallas_call(
        _inproj_kernel,
        grid=(t // tm, n // tn),
        in_specs=[
            pl.BlockSpec((tm, d), lambda i, j: (i, 0)),
            pl.BlockSpec((1, d), lambda i, j: (0, 0)),
            pl.BlockSpec((d, tn), lambda i, j: (0, j)),
        ],
        out_specs=pl.BlockSpec((tm, tn), lambda i, j: (i, j)),
        out_shape=jax.ShapeDtypeStruct((t, n), F32),
        scratch_shapes=[pltpu.VMEM((tm, d), BF16)],
        compiler_params=_cparams(("parallel", "arbitrary")),
    )(x2, norm_w, w_all)


def _gdn_prep_kernel(q_ref, k_ref, v_ref, qh_ref, kh_ref, vh_ref, sm_ref, cw_ref, al_ref, dt_ref,
                     qo_ref, ko_ref, vo_ref, g_ref, b_ref, *, seq_tiles):
    tm = q_ref.shape[0]
    first = (pl.program_id(0) % seq_tiles) == 0
    row8 = lax.broadcasted_iota(jnp.int32, (8, 1), 0)

    def conv_silu(x_ref, halo_ref, part):
        x = x_ref[...]
        halo = jnp.where(first, 0.0, halo_ref[...])
        w = cw_ref[:, part * x.shape[1]:(part + 1) * x.shape[1]]
        acc = x * w[GDN_CONV - 1:GDN_CONV, :]
        for d in range(1, GDN_CONV):
            xs = pltpu.roll(x, d, axis=0)
            hs = pltpu.roll(halo, d, axis=0)
            head = jnp.where(row8 < d, hs, xs[0:8, :])
            xs = jnp.concatenate([head, xs[8:, :]], axis=0)
            acc = acc + xs * w[GDN_CONV - 1 - d:GDN_CONV - d, :]
        return _silu(acc)

    def l2norm_heads(y, scale):
        outs = []
        for h in range(GDN_HEADS):
            yh = y[:, h * GDN_DK:(h + 1) * GDN_DK]
            ss = jnp.sum(yh * yh, axis=-1, keepdims=True)
            outs.append(yh * (lax.rsqrt(ss + NORM_EPS) * scale))
        return jnp.concatenate(outs, axis=1)

    qo_ref[...] = l2norm_heads(conv_silu(q_ref, qh_ref, 0), GDN_DK ** -0.5)
    ko_ref[...] = l2norm_heads(conv_silu(k_ref, kh_ref, 1), 1.0)
    vo_ref[...] = conv_silu(v_ref, vh_ref, 2)

    sm = sm_ref[...]
    z = sm + dt_ref[...]
    softplus = jnp.maximum(z, 0.0) + jnp.log1p(jnp.exp(-jnp.abs(z)))
    g = -jnp.exp(al_ref[...]) * softplus
    beta = _sigmoid(sm)
    r = lax.broadcasted_iota(jnp.int32, (tm, tm), 0)
    c = lax.broadcasted_iota(jnp.int32, (tm, tm), 1)
    tri = jnp.where((r // GDN_CHUNK == c // GDN_CHUNK) & (c <= r), 1.0, 0.0).astype(F32)
    gc = _dot_hi(tri, g)
    er = lax.broadcasted_iota(jnp.int32, (LANES, GDN_HEADS * LANES), 0)
    ec = lax.broadcasted_iota(jnp.int32, (LANES, GDN_HEADS * LANES), 1)
    sel_g = jnp.where(ec // LANES == er, 1.0, 0.0).astype(F32)
    sel_b = jnp.where(ec // LANES + GDN_HEADS == er, 1.0, 0.0).astype(F32)
    g_ref[...] = _dot_hi(gc, sel_g)
    b_ref[...] = _dot_hi(beta, sel_b)


def _gdn_prep(proj, conv_w, a_log_pad, dt_pad, off, s, tm):
    t = proj.shape[0]
    w = GDN_HEADS * GDN_DK
    cq, cs = off["qkv"] // w, off["small_b"] // LANES
    hb = tm // 8

    def halo(col):
        return pl.BlockSpec((8, w), lambda i: (jnp.maximum(i * hb - 1, 0), col))

    out = jax.ShapeDtypeStruct((t, w), F32)
    return pl.pallas_call(
        functools.partial(_gdn_prep_kernel, seq_tiles=s // tm),
        grid=(t // tm,),
        in_specs=[
            pl.BlockSpec((tm, w), lambda i: (i, cq)),
            pl.BlockSpec((tm, w), lambda i: (i, cq + 1)),
            pl.BlockSpec((tm, w), lambda i: (i, cq + 2)),
            halo(cq), halo(cq + 1), halo(cq + 2),
            pl.BlockSpec((tm, LANES), lambda i: (i, cs)),
            pl.BlockSpec((GDN_CONV, 3 * w), lambda i: (0, 0)),
            pl.BlockSpec((1, LANES), lambda i: (0, 0)),
            pl.BlockSpec((1, LANES), lambda i: (0, 0)),
        ],
        out_specs=[pl.BlockSpec((tm, w), lambda i: (i, 0))] * 5,
        out_shape=[out] * 5,
        compiler_params=_cparams(("parallel",)),
    )(proj, proj, proj, proj, proj, proj, proj, conv_w, a_log_pad, dt_pad)


def _unit_lower_inverse(a):
    c = a.shape[0]
    r = lax.broadcasted_iota(jnp.int32, (c, c), 0)
    col = lax.broadcasted_iota(jnp.int32, (c, c), 1)
    b = -a
    p = jnp.where(r == col, 1.0, 0.0).astype(F32) + b
    q = _dot_hi(b, b)
    n = 2
    while 2 * n < c:
        p = p + _dot_hi(q, p)
        q = _dot_hi(q, q)
        n *= 2
    return p + _dot_hi(q, p)


def _gdn_scan_kernel(q_ref, k_ref, v_ref, g_ref, b_ref, z_ref, nw_ref, o_ref, s_ref, *, heads, chunks):
    @pl.when(pl.program_id(2) == 0)
    def _():
        s_ref[...] = jnp.zeros_like(s_ref)

    cc = GDN_CHUNK
    r = lax.broadcasted_iota(jnp.int32, (cc, cc), 0)
    col = lax.broadcasted_iota(jnp.int32, (cc, cc), 1)
    causal = col <= r
    strict = col < r

    def chunk_body(ci, carry):
        rows = pl.ds(pl.multiple_of(ci * cc, cc), cc)
        for h in range(heads):
            cols = slice(h * GDN_DK, (h + 1) * GDN_DK)
            q = q_ref[rows, cols]
            k = k_ref[rows, cols]
            v = v_ref[rows, cols]
            g = g_ref[rows, cols]
            beta = b_ref[rows, cols]
            eg = jnp.exp(g)
            kb = k * beta
            k16 = k.astype(BF16)
            kbk = _dot_nt(kb.astype(BF16), k16)
            qk = _dot_nt(q.astype(BF16), k16)
            g_row = jnp.transpose(jnp.concatenate([g, g], axis=0))[0:cc, 0:cc]
            g_col = g[:, 0:cc]
            decay = jnp.where(causal, jnp.exp(jnp.where(causal, g_col - g_row, 0.0)), 0.0)
            a_mat = jnp.where(strict, kbk * decay, 0.0)
            attn = qk * decay
            t16 = _unit_lower_inverse(a_mat).astype(BF16)
            u = _dot(t16, (v * beta).astype(BF16))
            w = _dot(t16, (kb * eg).astype(BF16))
            s16 = s_ref[h].astype(BF16)
            v_new = u - _dot(w.astype(BF16), s16)
            vn16 = v_new.astype(BF16)
            o = _dot((q * eg).astype(BF16), s16) + _dot(attn.astype(BF16), vn16)
            g_last = g[cc - 1:cc, :]
            k_dec = k * jnp.exp(g_last - g)
            s_ref[h] = s_ref[h] * jnp.exp(g_last) + _dot_tn(k_dec.astype(BF16), vn16)
            ms = jnp.mean(o * o, axis=-1, keepdims=True)
            on = o * lax.rsqrt(ms + NORM_EPS) * nw_ref[...]
            o_ref[rows, cols] = (on * _silu(z_ref[rows, cols])).astype(o_ref.dtype)
        return carry

    lax.fori_loop(0, chunks, chunk_body, 0)


def _gdn_scan(qn, kn, vn, gcb, betab, proj, gdn_norm_w, off, b, s, heads, chunks):
    t, w = qn.shape
    rows = chunks * GDN_CHUNK
    bw = heads * GDN_DK
    nblk = s // rows
    cz = off["z_a"] // bw

    def tok(col0):
        return pl.BlockSpec((rows, bw), lambda bi, hi, ci: (bi * nblk + ci, col0 + hi))

    return pl.pallas_call(
        functools.partial(_gdn_scan_kernel, heads=heads, chunks=chunks),
        grid=(b, GDN_HEADS // heads, nblk),
        in_specs=[tok(0), tok(0), tok(0), tok(0), tok(0), tok(cz),
                  pl.BlockSpec((1, GDN_DV), lambda bi, hi, ci: (0, 0))],
        out_specs=tok(0),
        out_shape=jax.ShapeDtypeStruct((t, w), BF16),
        scratch_shapes=[pltpu.VMEM((heads, GDN_DK, GDN_DV), F32)],
        compiler_params=_cparams(("arbitrary", "arbitrary", "arbitrary")),
    )(qn, kn, vn, gcb, betab, proj, gdn_norm_w)


def _mla_prep_kernel(cq_ref, ckv_ref, kr_ref, pos_ref, qnw_ref, kvnw_ref, wq_ref, wkv_ref, invf_ref,
                     qn_ref, qr_ref, kn_ref, kro_ref, v_ref, *, qscale):
    def rms(x, w):
        ms = jnp.mean(x * x, axis=-1, keepdims=True)
        return (x * lax.rsqrt(ms + NORM_EPS) * w).astype(BF16)

    hw = MLA_HEADS * LANES
    qall = _dot(rms(cq_ref[...], qnw_ref[...]), wq_ref[...])
    kv = _dot(rms(ckv_ref[...], kvnw_ref[...]), wkv_ref[...])

    ang = pos_ref[...] * invf_ref[...]
    cos, sin = jnp.cos(ang), jnp.sin(ang)
    lane = lax.broadcasted_iota(jnp.int32, ang.shape, 1)
    half = MLA_ROPE // 2
    cm = jnp.where(lane < MLA_ROPE, cos, 0.0)
    sm = jnp.where(lane < MLA_ROPE, 0.0, jnp.where(lane < MLA_ROPE + half, -sin, sin))

    def rope(x):
        return x * cm + pltpu.roll(x * sm, MLA_ROPE, axis=1)

    qn_ref[...] = (qall[:, :hw] * qscale).astype(BF16)
    for h in range(MLA_HEADS):
        cols = slice(hw + h * LANES, hw + (h + 1) * LANES)
        qr_ref[:, h * LANES:(h + 1) * LANES] = (rope(qall[:, cols]) * qscale).astype(BF16)
    kn_ref[...] = kv[:, :hw].astype(BF16)
    v_ref[...] = kv[:, hw:].astype(BF16)
    kro_ref[...] = rope(kr_ref[...]).astype(BF16)


def _mla_prep(proj, pos_col, q_norm_w, kv_norm_w, wq, wkv, invf, off, tm, qscale):
    t = proj.shape[0]
    rq, rkv = wq.shape[0], wkv.shape[0]
    hw = MLA_HEADS * LANES
    full = lambda a: pl.BlockSpec(a.shape, lambda i: (0, 0))
    big = jax.ShapeDtypeStruct((t, hw), BF16)
    return pl.pallas_call(
        functools.partial(_mla_prep_kernel, qscale=qscale),
        grid=(t // tm,),
        in_specs=[
            pl.BlockSpec((tm, rq), lambda i: (i, off["c_q"] // rq)),
            pl.BlockSpec((tm, rkv), lambda i: (i, off["c_kv"] // rkv)),
            pl.BlockSpec((tm, LANES), lambda i: (i, off["small_a"] // LANES)),
            pl.BlockSpec((tm, 1), lambda i: (i, 0)),
            full(q_norm_w), full(kv_norm_w), full(wq), full(wkv), full(invf),
        ],
        out_specs=[
            pl.BlockSpec((tm, hw), lambda i: (i, 0)),
            pl.BlockSpec((tm, hw), lambda i: (i, 0)),
            pl.BlockSpec((tm, hw), lambda i: (i, 0)),
            pl.BlockSpec((tm, LANES), lambda i: (i, 0)),
            pl.BlockSpec((tm, hw), lambda i: (i, 0)),
        ],
        out_shape=[big, big, big, jax.ShapeDtypeStruct((t, LANES), BF16), big],
        compiler_params=_cparams(("parallel",)),
    )(proj, proj, proj, pos_col, q_norm_w, kv_norm_w, wq, wkv, invf)


def _flash_kernel(qn_ref, qr_ref, kn_ref, kr_ref, v_ref, z_ref, o_ref, acc_ref, *, tk):
    tq = qn_ref.shape[0]
    qi = pl.program_id(2)
    q = jnp.concatenate([qn_ref[...], qr_ref[...]], axis=1)
    acc_ref[...] = jnp.zeros_like(acc_ref)
    neg = -1e30

    def step(j, m, l, masked):
        rows = pl.ds(pl.multiple_of(j * tk, tk), tk)
        kb = jnp.concatenate([kn_ref[rows, :], kr_ref[rows, :]], axis=1)
        s = _dot_nt(q, kb)
        if masked:
            qpos = qi * tq + lax.broadcasted_iota(jnp.int32, (tq, tk), 0)
            kpos = j * tk + lax.broadcasted_iota(jnp.int32, (tq, tk), 1)
            s = jnp.where(kpos <= qpos, s, neg)
        m_new = jnp.maximum(m, jnp.max(s, axis=-1, keepdims=True))
        alpha = jnp.exp2(m - m_new)
        p = jnp.exp2(s - m_new)
        l = alpha * l + jnp.sum(p, axis=-1, keepdims=True)
        acc_ref[...] = alpha * acc_ref[...] + _dot(p.astype(BF16), v_ref[rows, :])
        return m_new, l

    m0 = jnp.full((tq, 1), neg, F32)
    l0 = jnp.zeros((tq, 1), F32)
    ratio = tq // tk
    m, l = lax.fori_loop(0, qi * ratio, lambda j, c: step(j, c[0], c[1], False), (m0, l0))
    for d in range(ratio):
        m, l = step(qi * ratio + d, m, l, True)
    o = acc_ref[...] / l
    o_ref[...] = (o * _silu(z_ref[...])).astype(o_ref.dtype)


def _flash(qn, qr, kn, kr, v, proj, off, b, s, tq, tk):
    t = qn.shape[0]
    nq = s // tq
    cz = off["z_b"] // LANES
    return pl.pallas_call(
        functools.partial(_flash_kernel, tk=tk),
        grid=(b, MLA_HEADS, nq),
        in_specs=[
            pl.BlockSpec((tq, LANES), lambda bi, h, i: (bi * nq + i, h)),
            pl.BlockSpec((tq, LANES), lambda bi, h, i: (bi * nq + i, h)),
            pl.BlockSpec((s, LANES), lambda bi, h, i: (bi, h)),
            pl.BlockSpec((s, LANES), lambda bi, h, i: (bi, 0)),
            pl.BlockSpec((s, LANES), lambda bi, h, i: (bi, h)),
            pl.BlockSpec((tq, LANES), lambda bi, h, i: (bi * nq + i, cz + h)),
        ],
        out_specs=pl.BlockSpec((tq, LANES), lambda bi, h, i: (bi * nq + i, h)),
        out_shape=jax.ShapeDtypeStruct((t, MLA_HEADS * MLA_V), BF16),
        scratch_shapes=[pltpu.VMEM((tq, MLA_V), F32)],
        compiler_params=_cparams(("parallel", "parallel", "arbitrary")),
    )(qn, qr, kn, kr, v, proj)


def _out_kernel(oa_ref, ob_ref, ga_ref, gb_ref, x_ref, pa_ref, pb_ref, wo_ref, fw_ref, y_ref):
    ya = _dot(oa_ref[...], pa_ref[...])
    yb = _dot(ob_ref[...], pb_ref[...])
    merged = _sigmoid(ga_ref[...]) * ya + _sigmoid(gb_ref[...]) * yb
    r = x_ref[...] + _dot(merged.astype(BF16), wo_ref[...])
    ms = jnp.mean(r * r, axis=-1, keepdims=True)
    y_ref[...] = r * lax.rsqrt(ms + NORM_EPS) * fw_ref[...]


def _out_stage(oa, ob, proj, x2, pa, pb, wo, fw, off, tm):
    t, d = x2.shape
    wv = oa.shape[1]
    const = lambda a: pl.BlockSpec(a.shape, lambda i: (0, 0), pipeline_mode=pl.Buffered(1))
    return pl.pallas_call(
        _out_kernel,
        grid=(t // tm,),
        in_specs=[
            pl.BlockSpec((tm, wv), lambda i: (i, 0)),
            pl.BlockSpec((tm, wv), lambda i: (i, 0)),
            pl.BlockSpec((tm, d), lambda i: (i, off["gate_a"] // d)),
            pl.BlockSpec((tm, d), lambda i: (i, off["gate_b"] // d)),
            pl.BlockSpec((tm, d), lambda i: (i, 0)),
            const(pa), const(pb), const(wo), const(fw),
        ],
        out_specs=pl.BlockSpec((tm, d), lambda i: (i, 0)),
        out_shape=jax.ShapeDtypeStruct((t, d), F32),
        compiler_params=_cparams(("parallel",)),
    )(oa, ob, proj, proj, x2, pa, pb, wo, fw)


def _layer(x2, pos_col, b, s, norm_w, w_in, conv_w, a_log, dt_bias, gdn_norm_w, q_norm_w, w_uq,
           kv_norm_w, w_ukv, proj_a, proj_b, w_out, out_norm_w):
    d = x2.shape[1]
    kw = GDN_HEADS * GDN_DK
    vw = GDN_HEADS * GDN_DV
    q_rank, kv_rank = w_uq.shape[0], w_ukv.shape[0]
    mv = MLA_HEADS * MLA_V
    widths = (2 * kw + vw, vw, GDN_HEADS, GDN_HEADS, q_rank, kv_rank, MLA_ROPE, mv, d, d)
    starts = [0]
    for wd in widths:
        starts.append(starts[-1] + wd)
    sl = lambda i: w_in[:, starts[i]:starts[i + 1]]
    w_qkv, w_za, w_alpha, w_beta, w_cq, w_ckv, w_kr, w_zb, w_ga, w_gb = (sl(i) for i in range(10))
    half = MLA_ROPE // 2
    w_kr1, w_kr2 = w_kr[:, :half], w_kr[:, half:]
    small_a = jnp.concatenate([w_kr1, w_kr2, w_kr2, w_kr1], axis=1)
    small_b = jnp.concatenate(
        [w_alpha, w_beta, jnp.zeros((d, LANES - 2 * GDN_HEADS), w_in.dtype)], axis=1)
    groups = (("gate_a", w_ga), ("gate_b", w_gb), ("qkv", w_qkv), ("z_a", w_za), ("z_b", w_zb),
              ("c_q", w_cq), ("c_kv", w_ckv), ("small_a", small_a), ("small_b", small_b))
    off, acc = {}, 0
    for name, wg in groups:
        off[name] = acc
        acc += wg.shape[1]
    w_all = jnp.concatenate([wg for _, wg in groups], axis=1).astype(BF16)

    t = x2.shape[0]
    tm1 = min(512, t)
    tn1 = 1024 if acc % 1024 == 0 else LANES
    proj = _inproj(x2, norm_w.reshape(1, d), w_all, tm1, tn1)

    pad = lambda a: jnp.pad(a.reshape(1, -1), ((0, 0), (0, LANES - a.shape[-1])))
    dt_pad = pad(dt_bias)
    tm2 = min(256, s)
    qn, kn, vn, gcb, betab = _gdn_prep(proj, conv_w, pad(a_log), dt_pad, off, s, tm2)
    chunks = min(4, s // GDN_CHUNK)
    oa = _gdn_scan(qn, kn, vn, gcb, betab, proj, gdn_norm_w.reshape(1, GDN_DV), off, b, s, 4, chunks)

    wq3 = w_uq.reshape(q_rank, MLA_HEADS, MLA_QK)
    wq_nope = wq3[:, :, :MLA_NOPE].reshape(q_rank, MLA_HEADS * MLA_NOPE)
    r1, r2 = wq3[:, :, MLA_NOPE:MLA_NOPE + half], wq3[:, :, MLA_NOPE + half:]
    wq_rope = jnp.concatenate([r1, r2, r2, r1], axis=2).reshape(q_rank, MLA_HEADS * LANES)
    wq = jnp.concatenate([wq_nope, wq_rope], axis=1).astype(BF16)
    wkv3 = w_ukv.reshape(kv_rank, MLA_HEADS, MLA_NOPE + MLA_V)
    wkv = jnp.concatenate([wkv3[:, :, :MLA_NOPE].reshape(kv_rank, -1),
                           wkv3[:, :, MLA_NOPE:].reshape(kv_rank, -1)], axis=1).astype(BF16)
    inv_freq = ROPE_THETA ** (-jnp.arange(0, MLA_ROPE, 2, dtype=F32) / MLA_ROPE)
    invf = jnp.tile(inv_freq, LANES // half).reshape(1, LANES)
    qscale = math.log2(math.e) / math.sqrt(MLA_QK)
    tm4 = min(256, t)
    qnope, qrope, knope, krope, vb = _mla_prep(
        proj, pos_col, q_norm_w.reshape(1, -1), kv_norm_w.reshape(1, -1), wq, wkv, invf, off, tm4, qscale)
    tq = min(512, s)
    ob = _flash(qnope, qrope, knope, krope, vb, proj, off, b, s, tq, tq)

    tm6 = min(256, t)
    return _out_stage(oa, ob, proj, x2, proj_a.astype(BF16), proj_b.astype(BF16), w_out.astype(BF16),
                      out_norm_w.reshape(1, d), off, tm6)


def kernel(x, positions, norm_w, w_in, conv_w, a_log, dt_bias, gdn_norm_w, q_norm_w, w_uq, kv_norm_w,
           w_ukv, proj_a, proj_b, w_out, final_norm_w):
    b, s, d = x.shape
    depth = norm_w.shape[0]
    assert depth == 1, "the final norm is fused into the last layer's output stage"
    x2 = x.reshape(b * s, d)
    pos_col = positions.astype(F32).reshape(b * s, 1)
    y = _layer(x2, pos_col, b, s, norm_w[0], w_in[0], conv_w[0], a_log[0], dt_bias[0], gdn_norm_w[0],
               q_norm_w[0], w_uq[0], kv_norm_w[0], w_ukv[0], proj_a[0], proj_b[0], w_out[0], final_norm_w)
    return y.reshape(b, s, d)
```

```python
import functools
import math

import jax
import jax.numpy as jnp
from jax import lax
from jax.experimental import pallas as pl
from jax.experimental.pallas import tpu as pltpu

F32 = jnp.float32
BF16 = jnp.bfloat16
HIGHEST = lax.Precision.HIGHEST

LANES = 128
NORM_EPS = 1e-6
ROPE_THETA = 10000.0

GDN_HEADS = 8
GDN_DK = 128
GDN_DV = 128
GDN_CONV = 4
GDN_CHUNK = 64
MLA_HEADS = 8
MLA_NOPE = 128
MLA_ROPE = 64
MLA_V = 128
MLA_QK = MLA_NOPE + MLA_ROPE

VMEM_LIMIT = 56 * 1024 * 1024


def _cparams(sem):
    return pltpu.CompilerParams(dimension_semantics=sem, vmem_limit_bytes=VMEM_LIMIT)


def _silu(x):
    return x / (1.0 + jnp.exp(-x))


def _sigmoid(x):
    return 1.0 / (1.0 + jnp.exp(-x))


def _dot(a, b):
    return jnp.dot(a, b, preferred_element_type=F32)


def _dot_nt(a, b):
    return lax.dot_general(a, b, (((1,), (1,)), ((), ())), preferred_element_type=F32)


def _dot_tn(a, b):
    return lax.dot_general(a, b, (((0,), (0,)), ((), ())), preferred_element_type=F32)


def _dot_hi(a, b):
    return jnp.dot(a, b, preferred_element_type=F32, precision=HIGHEST)


def _inproj_kernel(x_ref, nw_ref, w_ref, o_ref, h_ref):
    @pl.when(pl.program_id(1) == 0)
    def _():
        x = x_ref[...]
        ms = jnp.mean(x * x, axis=-1, keepdims=True)
        h_ref[...] = (x * lax.rsqrt(ms + NORM_EPS) * nw_ref[...]).astype(BF16)

    o_ref[...] = _dot(h_ref[...], w_ref[...])


def _inproj(x2, norm_w, w_all, tm, tn):
    t, d = x2.shape
    n = w_all.shape[1]
    return pl.pallas_call(
        _inproj_kernel,
        grid=(t // tm, n // tn),
        in_specs=[
            pl.BlockSpec((tm, d), lambda i, j: (i, 0)),
            pl.BlockSpec((1, d), lambda i, j: (0, 0)),
            pl.BlockSpec((d, tn), lambda i, j: (0, j)),
        ],
        out_specs=pl.BlockSpec((tm, tn), lambda i, j: (i, j)),
        out_shape=jax.ShapeDtypeStruct((t, n), F32),
        scratch_shapes=[pltpu.VMEM((tm, d), BF16)],
        compiler_params=_cparams(("parallel", "arbitrary")),
    )(x2, norm_w, w_all)


def _gdn_prep_kernel(q_ref, k_ref, v_ref, qh_ref, kh_ref, vh_ref, sm_ref, cw_ref, al_ref, dt_ref,
                     qo_ref, ko_ref, vo_ref, g_ref, b_ref, *, seq_tiles):
    tm = q_ref.shape[0]
    first = (pl.program_id(0) % seq_tiles) == 0
    row8 = lax.broadcasted_iota(jnp.int32, (8, 1), 0)

    def conv_silu(x_ref, halo_ref, part):
        x = x_ref[...]
        halo = jnp.where(first, 0.0, halo_ref[...])
        w = cw_ref[:, part * x.shape[1]:(part + 1) * x.shape[1]]
        acc = x * w[GDN_CONV - 1:GDN_CONV, :]
        for d in range(1, GDN_CONV):
            xs = pltpu.roll(x, d, axis=0)
            hs = pltpu.roll(halo, d, axis=0)
            head = jnp.where(row8 < d, hs, xs[0:8, :])
            xs = jnp.concatenate([head, xs[8:, :]], axis=0)
            acc = acc + xs * w[GDN_CONV - 1 - d:GDN_CONV - d, :]
        return _silu(acc)

    def l2norm_heads(y, scale):
        outs = []
        for h in range(GDN_HEADS):
            yh = y[:, h * GDN_DK:(h + 1) * GDN_DK]
            ss = jnp.sum(yh * yh, axis=-1, keepdims=True)
            outs.append(yh * (lax.rsqrt(ss + NORM_EPS) * scale))
        return jnp.concatenate(outs, axis=1)

    qo_ref[...] = l2norm_heads(conv_silu(q_ref, qh_ref, 0), GDN_DK ** -0.5)
    ko_ref[...] = l2norm_heads(conv_silu(k_ref, kh_ref, 1), 1.0)
    vo_ref[...] = conv_silu(v_ref, vh_ref, 2)

    sm = sm_ref[...]
    z = sm + dt_ref[...]
    softplus = jnp.maximum(z, 0.0) + jnp.log1p(jnp.exp(-jnp.abs(z)))
    g = -jnp.exp(al_ref[...]) * softplus
    beta = _sigmoid(sm)
    r = lax.broadcasted_iota(jnp.int32, (tm, tm), 0)
    c = lax.broadcasted_iota(jnp.int32, (tm, tm), 1)
    tri = jnp.where((r // GDN_CHUNK == c // GDN_CHUNK) & (c <= r), 1.0, 0.0).astype(F32)
    gc = _dot_hi(tri, g)
    er = lax.broadcasted_iota(jnp.int32, (LANES, GDN_HEADS * LANES), 0)
    ec = lax.broadcasted_iota(jnp.int32, (LANES, GDN_HEADS * LANES), 1)
    sel_g = jnp.where(ec // LANES == er, 1.0, 0.0).astype(F32)
    sel_b = jnp.where(ec // LANES + GDN_HEADS == er, 1.0, 0.0).astype(F32)
    g_ref[...] = _dot_hi(gc, sel_g)
    b_ref[...] = _dot_hi(beta, sel_b)


def _gdn_prep(proj, conv_w, a_log_pad, dt_pad, off, s, tm):
    t = proj.shape[0]
    w = GDN_HEADS * GDN_DK
    cq, cs = off["qkv"] // w, off["small_b"] // LANES
    hb = tm // 8

    def halo(col):
        return pl.BlockSpec((8, w), lambda i: (jnp.maximum(i * hb - 1, 0), col))

    out = jax.ShapeDtypeStruct((t, w), F32)
    return pl.pallas_call(
        functools.partial(_gdn_prep_kernel, seq_tiles=s // tm),
        grid=(t // tm,),
        in_specs=[
            pl.BlockSpec((tm, w), lambda i: (i, cq)),
            pl.BlockSpec((tm, w), lambda i: (i, cq + 1)),
            pl.BlockSpec((tm, w), lambda i: (i, cq + 2)),
            halo(cq), halo(cq + 1), halo(cq + 2),
            pl.BlockSpec((tm, LANES), lambda i: (i, cs)),
            pl.BlockSpec((GDN_CONV, 3 * w), lambda i: (0, 0)),
            pl.BlockSpec((1, LANES), lambda i: (0, 0)),
            pl.BlockSpec((1, LANES), lambda i: (0, 0)),
        ],
        out_specs=[pl.BlockSpec((tm, w), lambda i: (i, 0))] * 5,
        out_shape=[out] * 5,
        compiler_params=_cparams(("parallel",)),
    )(proj, proj, proj, proj, proj, proj, proj, conv_w, a_log_pad, dt_pad)


def _split_bf16(x):
    hi = x.astype(BF16)
    lo = (x - hi.astype(F32)).astype(BF16)
    return hi, lo


def _unit_lower_inverse_wide(neg_a_wide):
    c = neg_a_wide[0].shape[0]
    row = lax.broadcasted_iota(jnp.int32, (c, 2 * c), 0)
    lane = lax.broadcasted_iota(jnp.int32, (c, 2 * c), 1)
    left = lane < c
    eye = jnp.where(row == lane, 1.0, 0.0)
    rs = [jnp.where(left, eye, na) for na in neg_a_wide]
    zeros = jnp.zeros((c, 2 * c), BF16)
    n = 1
    while n < c:
        nxt = []
        for r in rs:
            rh, rl = _split_bf16(r)
            bh = jnp.concatenate([zeros, rh], axis=0)
            bl = jnp.concatenate([zeros, rl], axis=0)
            x = _dot(rh, jnp.concatenate([bh, bl], axis=1))
            qr = x[:, :2 * c] + x[:, 2 * c:] + _dot(rl, bh)
            nxt.append(jnp.where(left, r + qr, qr))
        rs = nxt
        n *= 2
    return rs


def _gdn_scan_kernel(q_ref, k_ref, v_ref, g_ref, b_ref, z_ref, nw_ref, o_ref,
                     s_ref, u_ref, w_ref, qd_ref, at_ref, kdt_ref, *, chunks):
    @pl.when(pl.program_id(1) == 0)
    def _():
        s_ref[...] = jnp.zeros_like(s_ref)

    cc = GDN_CHUNK
    row = lax.broadcasted_iota(jnp.int32, (cc, 2 * cc), 0)
    key = lax.broadcasted_iota(jnp.int32, (cc, 2 * cc), 1) % cc
    causal = key <= row
    strict = key < row

    heads = range(GDN_HEADS)
    head_cols = [slice(h * GDN_DK, (h + 1) * GDN_DK) for h in heads]

    def factor_chunk(ci, carry):
        rows = pl.ds(pl.multiple_of(ci * cc, cc), cc)
        kk_qk, neg_a = [], []
        for cols in head_cols:
            k = k_ref[rows, cols]
            k16 = k.astype(BF16)
            kb16 = (k * b_ref[rows, cols]).astype(BF16)
            kk_qk.append(_dot_nt(jnp.concatenate([kb16, q_ref[rows, cols].astype(BF16)], axis=0),
                                 jnp.concatenate([k16, k16], axis=0)))
        for h, cols in enumerate(head_cols):
            g = g_ref[rows, cols]
            g_row = jnp.transpose(jnp.concatenate([g, g], axis=0))[0:cc, :]
            decay = jnp.where(causal, jnp.exp(jnp.where(causal, g - g_row, 0.0)), 0.0)
            neg_a.append(jnp.where(strict, -(kk_qk[h][0:cc, :] * decay), 0.0))
            at_ref[rows, h * GDN_DK:h * GDN_DK + cc] = (kk_qk[h][cc:, 0:cc] * decay[:, 0:cc]).astype(BF16)
        t_wide = _unit_lower_inverse_wide(neg_a)
        for h, cols in enumerate(head_cols):
            k = k_ref[rows, cols]
            g = g_ref[rows, cols]
            beta = b_ref[rows, cols]
            eg = jnp.exp(g)
            rhs = jnp.concatenate([(v_ref[rows, cols] * beta).astype(BF16), (k * beta * eg).astype(BF16)], axis=1)
            uw = _dot(t_wide[h][:, 0:cc].astype(BF16), rhs)
            u_ref[rows, cols] = uw[:, :GDN_DV]
            w_ref[rows, cols] = uw[:, GDN_DV:].astype(BF16)
            qd_ref[rows, cols] = (q_ref[rows, cols] * eg).astype(BF16)
            k_dec = k * jnp.exp(g[cc - 1:cc, :] - g)
            kdt_ref[ci * GDN_HEADS + h] = jnp.transpose(
                jnp.concatenate([k_dec, k_dec], axis=0))[:, 0:cc].astype(BF16)
        return carry

    def scan_chunk(ci, carry):
        rows = pl.ds(pl.multiple_of(ci * cc, cc), cc)
        tail = pl.ds(pl.multiple_of(ci * cc + cc - 8, 8), 8)
        ws_qs = [_dot(jnp.concatenate([w_ref[rows, cols], qd_ref[rows, cols]], axis=0), s_ref[h].astype(BF16))
                 for h, cols in enumerate(head_cols)]
        vn16 = [(u_ref[rows, cols] - ws_qs[h][0:cc, :]).astype(BF16) for h, cols in enumerate(head_cols)]
        for h, cols in enumerate(head_cols):
            g_last = g_ref[tail, cols][7:8, :]
            s_ref[h] = s_ref[h] * jnp.exp(g_last) + _dot(kdt_ref[ci * GDN_HEADS + h], vn16[h])
        for h, cols in enumerate(head_cols):
            o = ws_qs[h][cc:, :] + _dot(at_ref[rows, h * GDN_DK:h * GDN_DK + cc], vn16[h])
            ms = jnp.mean(o * o, axis=-1, keepdims=True)
            on = o * lax.rsqrt(ms + NORM_EPS) * nw_ref[...]
            o_ref[rows, cols] = (on * _silu(z_ref[rows, cols])).astype(o_ref.dtype)
        return carry

    lax.fori_loop(0, chunks, factor_chunk, 0)
    lax.fori_loop(0, chunks, scan_chunk, 0)


def _gdn_scan(qn, kn, vn, gcb, betab, proj, gdn_norm_w, off, b, s, chunks):
    t, w = qn.shape
    rows = chunks * GDN_CHUNK
    nblk = s // rows
    cz = off["z_a"] // w

    def tok(col):
        return pl.BlockSpec((rows, w), lambda bi, ci: (bi * nblk + ci, col))

    return pl.pallas_call(
        functools.partial(_gdn_scan_kernel, chunks=chunks),
        grid=(b, nblk),
        in_specs=[tok(0), tok(0), tok(0), tok(0), tok(0), tok(cz),
                  pl.BlockSpec((1, GDN_DV), lambda bi, ci: (0, 0))],
        out_specs=tok(0),
        out_shape=jax.ShapeDtypeStruct((t, w), BF16),
        scratch_shapes=[
            pltpu.VMEM((GDN_HEADS, GDN_DK, GDN_DV), F32),
            pltpu.VMEM((rows, w), F32),
            pltpu.VMEM((rows, w), BF16),
            pltpu.VMEM((rows, w), BF16),
            pltpu.VMEM((rows, w), BF16),
            pltpu.VMEM((chunks * GDN_HEADS, GDN_DK, GDN_CHUNK), BF16),
        ],
        compiler_params=_cparams(("arbitrary", "arbitrary")),
    )(qn, kn, vn, gcb, betab, proj, gdn_norm_w)


def _mla_prep_kernel(cq_ref, ckv_ref, kr_ref, pos_ref, qnw_ref, kvnw_ref, wq_ref, wkv_ref, invf_ref,
                     qn_ref, qr_ref, kn_ref, kro_ref, v_ref, *, qscale):
    def rms(x, w):
        ms = jnp.mean(x * x, axis=-1, keepdims=True)
        return (x * lax.rsqrt(ms + NORM_EPS) * w).astype(BF16)

    hw = MLA_HEADS * LANES
    qall = _dot(rms(cq_ref[...], qnw_ref[...]), wq_ref[...])
    kv = _dot(rms(ckv_ref[...], kvnw_ref[...]), wkv_ref[...])

    ang = pos_ref[...] * invf_ref[...]
    cos, sin = jnp.cos(ang), jnp.sin(ang)
    lane = lax.broadcasted_iota(jnp.int32, ang.shape, 1)
    half = MLA_ROPE // 2
    cm = jnp.where(lane < MLA_ROPE, cos, 0.0)
    sm = jnp.where(lane < MLA_ROPE, 0.0, jnp.where(lane < MLA_ROPE + half, -sin, sin))

    def rope(x):
        return x * cm + pltpu.roll(x * sm, MLA_ROPE, axis=1)

    qn_ref[...] = (qall[:, :hw] * qscale).astype(BF16)
    for h in range(MLA_HEADS):
        cols = slice(hw + h * LANES, hw + (h + 1) * LANES)
        qr_ref[:, h * LANES:(h + 1) * LANES] = (rope(qall[:, cols]) * qscale).astype(BF16)
    kn_ref[...] = kv[:, :hw].astype(BF16)
    v_ref[...] = kv[:, hw:].astype(BF16)
    kro_ref[...] = rope(kr_ref[...]).astype(BF16)


def _mla_prep(proj, pos_col, q_norm_w, kv_norm_w, wq, wkv, invf, off, tm, qscale):
    t = proj.shape[0]
    rq, rkv = wq.shape[0], wkv.shape[0]
    hw = MLA_HEADS * LANES
    full = lambda a: pl.BlockSpec(a.shape, lambda i: (0, 0))
    big = jax.ShapeDtypeStruct((t, hw), BF16)
    return pl.pallas_call(
        functools.partial(_mla_prep_kernel, qscale=qscale),
        grid=(t // tm,),
        in_specs=[
            pl.BlockSpec((tm, rq), lambda i: (i, off["c_q"] // rq)),
            pl.BlockSpec((tm, rkv), lambda i: (i, off["c_kv"] // rkv)),
            pl.BlockSpec((tm, LANES), lambda i: (i, off["small_a"] // LANES)),
            pl.BlockSpec((tm, 1), lambda i: (i, 0)),
            full(q_norm_w), full(kv_norm_w), full(wq), full(wkv), full(invf),
        ],
        out_specs=[
            pl.BlockSpec((tm, hw), lambda i: (i, 0)),
            pl.BlockSpec((tm, hw), lambda i: (i, 0)),
            pl.BlockSpec((tm, hw), lambda i: (i, 0)),
            pl.BlockSpec((tm, LANES), lambda i: (i, 0)),
            pl.BlockSpec((tm, hw), lambda i: (i, 0)),
        ],
        out_shape=[big, big, big, jax.ShapeDtypeStruct((t, LANES), BF16), big],
        compiler_params=_cparams(("parallel",)),
    )(proj, proj, proj, pos_col, q_norm_w, kv_norm_w, wq, wkv, invf)


def _flash_kernel(qn_ref, qr_ref, kn_ref, kr_ref, v_ref, z_ref, o_ref, acc_ref, *, tk):
    tq = qn_ref.shape[0]
    qi = pl.program_id(2)
    q = jnp.concatenate([qn_ref[...], qr_ref[...]], axis=1)
    acc_ref[...] = jnp.zeros_like(acc_ref)
    neg = -1e30

    def step(j, m, l, masked):
        rows = pl.ds(pl.multiple_of(j * tk, tk), tk)
        kb = jnp.concatenate([kn_ref[rows, :], kr_ref[rows, :]], axis=1)
        s = _dot_nt(q, kb)
        if masked:
            qpos = qi * tq + lax.broadcasted_iota(jnp.int32, (tq, tk), 0)
            kpos = j * tk + lax.broadcasted_iota(jnp.int32, (tq, tk), 1)
            s = jnp.where(kpos <= qpos, s, neg)
        m_new = jnp.maximum(m, jnp.max(s, axis=-1, keepdims=True))
        alpha = jnp.exp2(m - m_new)
        p = jnp.exp2(s - m_new)
        l = alpha * l + jnp.sum(p, axis=-1, keepdims=True)
        acc_ref[...] = alpha * acc_ref[...] + _dot(p.astype(BF16), v_ref[rows, :])
        return m_new, l

    m0 = jnp.full((tq, 1), neg, F32)
    l0 = jnp.zeros((tq, 1), F32)
    ratio = tq // tk
    m, l = lax.fori_loop(0, qi * ratio, lambda j, c: step(j, c[0], c[1], False), (m0, l0))
    for d in range(ratio):
        m, l = step(qi * ratio + d, m, l, True)
    o = acc_ref[...] / l
    o_ref[...] = (o * _silu(z_ref[...])).astype(o_ref.dtype)


def _flash(qn, qr, kn, kr, v, proj, off, b, s, tq, tk):
    t = qn.shape[0]
    nq = s // tq
    cz = off["z_b"] // LANES
    return pl.pallas_call(
        functools.partial(_flash_kernel, tk=tk),
        grid=(b, MLA_HEADS, nq),
        in_specs=[
            pl.BlockSpec((tq, LANES), lambda bi, h, i: (bi * nq + i, h)),
            pl.BlockSpec((tq, LANES), lambda bi, h, i: (bi * nq + i, h)),
            pl.BlockSpec((s, LANES), lambda bi, h, i: (bi, h)),
            pl.BlockSpec((s, LANES), lambda bi, h, i: (bi, 0)),
            pl.BlockSpec((s, LANES), lambda bi, h, i: (bi, h)),
            pl.BlockSpec((tq, LANES), lambda bi, h, i: (bi * nq + i, cz + h)),
        ],
        out_specs=pl.BlockSpec((tq, LANES), lambda bi, h, i: (bi * nq + i, h)),
        out_shape=jax.ShapeDtypeStruct((t, MLA_HEADS * MLA_V), BF16),
        scratch_shapes=[pltpu.VMEM((tq, MLA_V), F32)],
        compiler_params=_cparams(("parallel", "parallel", "arbitrary")),
    )(qn, qr, kn, kr, v, proj)


def _out_kernel(oa_ref, ob_ref, ga_ref, gb_ref, x_ref, pa_ref, pb_ref, wo_ref, fw_ref, y_ref):
    ya = _dot(oa_ref[...], pa_ref[...])
    yb = _dot(ob_ref[...], pb_ref[...])
    merged = _sigmoid(ga_ref[...]) * ya + _sigmoid(gb_ref[...]) * yb
    r = x_ref[...] + _dot(merged.astype(BF16), wo_ref[...])
    ms = jnp.mean(r * r, axis=-1, keepdims=True)
    y_ref[...] = r * lax.rsqrt(ms + NORM_EPS) * fw_ref[...]


def _out_stage(oa, ob, proj, x2, pa, pb, wo, fw, off, tm):
    t, d = x2.shape
    wv = oa.shape[1]
    const = lambda a: pl.BlockSpec(a.shape, lambda i: (0, 0), pipeline_mode=pl.Buffered(1))
    return pl.pallas_call(
        _out_kernel,
        grid=(t // tm,),
        in_specs=[
            pl.BlockSpec((tm, wv), lambda i: (i, 0)),
            pl.BlockSpec((tm, wv), lambda i: (i, 0)),
            pl.BlockSpec((tm, d), lambda i: (i, off["gate_a"] // d)),
            pl.BlockSpec((tm, d), lambda i: (i, off["gate_b"] // d)),
            pl.BlockSpec((tm, d), lambda i: (i, 0)),
            const(pa), const(pb), const(wo), const(fw),
        ],
        out_specs=pl.BlockSpec((tm, d), lambda i: (i, 0)),
        out_shape=jax.ShapeDtypeStruct((t, d), F32),
        compiler_params=_cparams(("parallel",)),
    )(oa, ob, proj, proj, x2, pa, pb, wo, fw)


def _layer(x2, pos_col, b, s, norm_w, w_in, conv_w, a_log, dt_bias, gdn_norm_w, q_norm_w, w_uq,
           kv_norm_w, w_ukv, proj_a, proj_b, w_out, out_norm_w):
    d = x2.shape[1]
    kw = GDN_HEADS * GDN_DK
    vw = GDN_HEADS * GDN_DV
    q_rank, kv_rank = w_uq.shape[0], w_ukv.shape[0]
    mv = MLA_HEADS * MLA_V
    widths = (2 * kw + vw, vw, GDN_HEADS, GDN_HEADS, q_rank, kv_rank, MLA_ROPE, mv, d, d)
    starts = [0]
    for wd in widths:
        starts.append(starts[-1] + wd)
    sl = lambda i: w_in[:, starts[i]:starts[i + 1]]
    w_qkv, w_za, w_alpha, w_beta, w_cq, w_ckv, w_kr, w_zb, w_ga, w_gb = (sl(i) for i in range(10))
    half = MLA_ROPE // 2
    w_kr1, w_kr2 = w_kr[:, :half], w_kr[:, half:]
    small_a = jnp.concatenate([w_kr1, w_kr2, w_kr2, w_kr1], axis=1)
    small_b = jnp.concatenate(
        [w_alpha, w_beta, jnp.zeros((d, LANES - 2 * GDN_HEADS), w_in.dtype)], axis=1)
    groups = (("gate_a", w_ga), ("gate_b", w_gb), ("qkv", w_qkv), ("z_a", w_za), ("z_b", w_zb),
              ("c_q", w_cq), ("c_kv", w_ckv), ("small_a", small_a), ("small_b", small_b))
    off, acc = {}, 0
    for name, wg in groups:
        off[name] = acc
        acc += wg.shape[1]
    w_all = jnp.concatenate([wg for _, wg in groups], axis=1).astype(BF16)

    t = x2.shape[0]
    tm1 = min(512, t)
    tn1 = 1024 if acc % 1024 == 0 else LANES
    proj = _inproj(x2, norm_w.reshape(1, d), w_all, tm1, tn1)

    pad = lambda a: jnp.pad(a.reshape(1, -1), ((0, 0), (0, LANES - a.shape[-1])))
    dt_pad = pad(dt_bias)
    tm2 = min(256, s)
    qn, kn, vn, gcb, betab = _gdn_prep(proj, conv_w, pad(a_log), dt_pad, off, s, tm2)
    chunks = min(4, s // GDN_CHUNK)
    oa = _gdn_scan(qn, kn, vn, gcb, betab, proj, gdn_norm_w.reshape(1, GDN_DV), off, b, s, chunks)

    wq3 = w_uq.reshape(q_rank, MLA_HEADS, MLA_QK)
    wq_nope = wq3[:, :, :MLA_NOPE].reshape(q_rank, MLA_HEADS * MLA_NOPE)
    r1, r2 = wq3[:, :, MLA_NOPE:MLA_NOPE + half], wq3[:, :, MLA_NOPE + half:]
    wq_rope = jnp.concatenate([r1, r2, r2, r1], axis=2).reshape(q_rank, MLA_HEADS * LANES)
    wq = jnp.concatenate([wq_nope, wq_rope], axis=1).astype(BF16)
    wkv3 = w_ukv.reshape(kv_rank, MLA_HEADS, MLA_NOPE + MLA_V)
    wkv = jnp.concatenate([wkv3[:, :, :MLA_NOPE].reshape(kv_rank, -1),
                           wkv3[:, :, MLA_NOPE:].reshape(kv_rank, -1)], axis=1).astype(BF16)
    inv_freq = ROPE_THETA ** (-jnp.arange(0, MLA_ROPE, 2, dtype=F32) / MLA_ROPE)
    invf = jnp.tile(inv_freq, LANES // half).reshape(1, LANES)
    qscale = math.log2(math.e) / math.sqrt(MLA_QK)
    tm4 = min(256, t)
    qnope, qrope, knope, krope, vb = _mla_prep(
        proj, pos_col, q_norm_w.reshape(1, -1), kv_norm_w.reshape(1, -1), wq, wkv, invf, off, tm4, qscale)
    tq = min(512, s)
    ob = _flash(qnope, qrope, knope, krope, vb, proj, off, b, s, tq, tq)

    tm6 = min(256, t)
    return _out_stage(oa, ob, proj, x2, proj_a.astype(BF16), proj_b.astype(BF16), w_out.astype(BF16),
                      out_norm_w.reshape(1, d), off, tm6)


def kernel(x, positions, norm_w, w_in, conv_w, a_log, dt_bias, gdn_norm_w, q_norm_w, w_uq, kv_norm_w,
           w_ukv, proj_a, proj_b, w_out, final_norm_w):
    b, s, d = x.shape
    depth = norm_w.shape[0]
    assert depth == 1, "the final norm is fused into the last layer's output stage"
    x2 = x.reshape(b * s, d)
    pos_col = positions.astype(F32).reshape(b * s, 1)
    y = _layer(x2, pos_col, b, s, norm_w[0], w_in[0], conv_w[0], a_log[0], dt_bias[0], gdn_norm_w[0],
               q_norm_w[0], w_uq[0], kv_norm_w[0], w_ukv[0], proj_a[0], proj_b[0], w_out[0], final_norm_w)
    return y.reshape(b, s, d)
```

```python
import functools
import math

import jax
import jax.numpy as jnp
from jax import lax
from jax.experimental import pallas as pl
from jax.experimental.pallas import tpu as pltpu

F32 = jnp.float32
BF16 = jnp.bfloat16
HIGHEST = lax.Precision.HIGHEST

LANES = 128
SUBLANES = 8
BF16_ROWS = 16
NORM_EPS = 1e-6
ROPE_THETA = 10000.0

GDN_HEADS = 8
GDN_DK = 128
GDN_DV = 128
GDN_CONV = 4
GDN_CHUNK = 64
MLA_HEADS = 8
MLA_NOPE = 128
MLA_ROPE = 64
MLA_V = 128
MLA_QK = MLA_NOPE + MLA_ROPE

VMEM_LIMIT = 56 * 1024 * 1024


def _cparams(sem):
    return pltpu.CompilerParams(dimension_semantics=sem, vmem_limit_bytes=VMEM_LIMIT)


def _silu(x):
    return x / (1.0 + jnp.exp(-x))


def _sigmoid(x):
    return 1.0 / (1.0 + jnp.exp(-x))


def _dot(a, b):
    return jnp.dot(a, b, preferred_element_type=F32)


def _dot_nt(a, b):
    return lax.dot_general(a, b, (((1,), (1,)), ((), ())), preferred_element_type=F32)


def _dot_tn(a, b):
    return lax.dot_general(a, b, (((0,), (0,)), ((), ())), preferred_element_type=F32)


def _dot_hi(a, b):
    return jnp.dot(a, b, preferred_element_type=F32, precision=HIGHEST)


def _inproj_kernel(x_ref, nw_ref, w_ref, o_ref, h_ref):
    @pl.when(pl.program_id(1) == 0)
    def _():
        x = x_ref[...]
        ms = jnp.mean(x * x, axis=-1, keepdims=True)
        h_ref[...] = (x * lax.rsqrt(ms + NORM_EPS) * nw_ref[...]).astype(BF16)

    o_ref[...] = _dot(h_ref[...], w_ref[...])


def _inproj(x2, norm_w, w_all, tm, tn):
    t, d = x2.shape
    n = w_all.shape[1]
    return pl.pallas_call(
        _inproj_kernel,
        grid=(t // tm, n // tn),
        in_specs=[
            pl.BlockSpec((tm, d), lambda i, j: (i, 0)),
            pl.BlockSpec((1, d), lambda i, j: (0, 0)),
            pl.BlockSpec((d, tn), lambda i, j: (0, j)),
        ],
        out_specs=pl.BlockSpec((tm, tn), lambda i, j: (i, j)),
        out_shape=jax.ShapeDtypeStruct((t, n), F32),
        scratch_shapes=[pltpu.VMEM((tm, d), BF16)],
        compiler_params=_cparams(("parallel", "arbitrary")),
    )(x2, norm_w, w_all)


def _gdn_prep_kernel(q_ref, k_ref, v_ref, qh_ref, kh_ref, vh_ref, sm_ref, cw_ref, al_ref, dt_ref,
                     qo_ref, ko_ref, vo_ref, g_ref, b_ref, *, seq_tiles):
    tm = q_ref.shape[0]
    first = (pl.program_id(0) % seq_tiles) == 0
    row8 = lax.broadcasted_iota(jnp.int32, (8, 1), 0)

    def conv_silu(x_ref, halo_ref, part):
        x = x_ref[...]
        halo = jnp.where(first, 0.0, halo_ref[...])
        w = cw_ref[:, part * x.shape[1]:(part + 1) * x.shape[1]]
        acc = x * w[GDN_CONV - 1:GDN_CONV, :]
        for d in range(1, GDN_CONV):
            xs = pltpu.roll(x, d, axis=0)
            hs = pltpu.roll(halo, d, axis=0)
            head = jnp.where(row8 < d, hs, xs[0:8, :])
            xs = jnp.concatenate([head, xs[8:, :]], axis=0)
            acc = acc + xs * w[GDN_CONV - 1 - d:GDN_CONV - d, :]
        return _silu(acc)

    def l2norm_heads(y, scale):
        outs = []
        for h in range(GDN_HEADS):
            yh = y[:, h * GDN_DK:(h + 1) * GDN_DK]
            ss = jnp.sum(yh * yh, axis=-1, keepdims=True)
            outs.append(yh * (lax.rsqrt(ss + NORM_EPS) * scale))
        return jnp.concatenate(outs, axis=1)

    qo_ref[...] = l2norm_heads(conv_silu(q_ref, qh_ref, 0), GDN_DK ** -0.5)
    ko_ref[...] = l2norm_heads(conv_silu(k_ref, kh_ref, 1), 1.0)
    vo_ref[...] = conv_silu(v_ref, vh_ref, 2)

    sm = sm_ref[...]
    z = sm + dt_ref[...]
    softplus = jnp.maximum(z, 0.0) + jnp.log1p(jnp.exp(-jnp.abs(z)))
    g = -jnp.exp(al_ref[...]) * softplus
    beta = _sigmoid(sm)
    r = lax.broadcasted_iota(jnp.int32, (tm, tm), 0)
    c = lax.broadcasted_iota(jnp.int32, (tm, tm), 1)
    tri = jnp.where((r // GDN_CHUNK == c // GDN_CHUNK) & (c <= r), 1.0, 0.0).astype(F32)
    gc = _dot_hi(tri, g)
    er = lax.broadcasted_iota(jnp.int32, (LANES, GDN_HEADS * LANES), 0)
    ec = lax.broadcasted_iota(jnp.int32, (LANES, GDN_HEADS * LANES), 1)
    sel_g = jnp.where(ec // LANES == er, 1.0, 0.0).astype(F32)
    sel_b = jnp.where(ec // LANES + GDN_HEADS == er, 1.0, 0.0).astype(F32)
    g_ref[...] = _dot_hi(gc, sel_g)
    b_ref[...] = _dot_hi(beta, sel_b)


def _gdn_prep(proj, conv_w, a_log_pad, dt_pad, off, s, tm):
    t = proj.shape[0]
    w = GDN_HEADS * GDN_DK
    cq, cs = off["qkv"] // w, off["small_b"] // LANES
    hb = tm // 8

    def halo(col):
        return pl.BlockSpec((8, w), lambda i: (jnp.maximum(i * hb - 1, 0), col))

    out = jax.ShapeDtypeStruct((t, w), F32)
    return pl.pallas_call(
        functools.partial(_gdn_prep_kernel, seq_tiles=s // tm),
        grid=(t // tm,),
        in_specs=[
            pl.BlockSpec((tm, w), lambda i: (i, cq)),
            pl.BlockSpec((tm, w), lambda i: (i, cq + 1)),
            pl.BlockSpec((tm, w), lambda i: (i, cq + 2)),
            halo(cq), halo(cq + 1), halo(cq + 2),
            pl.BlockSpec((tm, LANES), lambda i: (i, cs)),
            pl.BlockSpec((GDN_CONV, 3 * w), lambda i: (0, 0)),
            pl.BlockSpec((1, LANES), lambda i: (0, 0)),
            pl.BlockSpec((1, LANES), lambda i: (0, 0)),
        ],
        out_specs=[pl.BlockSpec((tm, w), lambda i: (i, 0))] * 5,
        out_shape=[out] * 5,
        compiler_params=_cparams(("parallel",)),
    )(proj, proj, proj, proj, proj, proj, proj, conv_w, a_log_pad, dt_pad)


def _split_bf16(x):
    hi = x.astype(BF16)
    lo = (x - hi.astype(F32)).astype(BF16)
    return hi, lo


def _unit_lower_inverse_wide(neg_a_wide):
    c = neg_a_wide[0].shape[0]
    row = lax.broadcasted_iota(jnp.int32, (c, 2 * c), 0)
    lane = lax.broadcasted_iota(jnp.int32, (c, 2 * c), 1)
    left = lane < c
    eye = jnp.where(row == lane, 1.0, 0.0)
    rs = [jnp.where(left, eye, na) for na in neg_a_wide]
    zeros = jnp.zeros((c, 2 * c), BF16)
    n = 1
    while n < c:
        nxt = []
        for r in rs:
            rh, rl = _split_bf16(r)
            bh = jnp.concatenate([zeros, rh], axis=0)
            bl = jnp.concatenate([zeros, rl], axis=0)
            x = _dot(rh, jnp.concatenate([bh, bl], axis=1))
            qr = x[:, :2 * c] + x[:, 2 * c:] + _dot(rl, bh)
            nxt.append(jnp.where(left, r + qr, qr))
        rs = nxt
        n *= 2
    return rs


def _gdn_scan_kernel(q_ref, k_ref, v_ref, g_ref, b_ref, z_ref, nw_ref, o_ref,
                     s_ref, u_ref, w_ref, qd_ref, at_ref, kdt_ref, *, chunks):
    @pl.when(pl.program_id(1) == 0)
    def _():
        s_ref[...] = jnp.zeros_like(s_ref)

    cc = GDN_CHUNK
    row = lax.broadcasted_iota(jnp.int32, (cc, 2 * cc), 0)
    key = lax.broadcasted_iota(jnp.int32, (cc, 2 * cc), 1) % cc
    causal = key <= row
    strict = key < row

    heads = range(GDN_HEADS)
    head_cols = [slice(h * GDN_DK, (h + 1) * GDN_DK) for h in heads]

    def factor_chunk(ci, carry):
        rows = pl.ds(pl.multiple_of(ci * cc, cc), cc)
        kk_qk, neg_a = [], []
        for cols in head_cols:
            k = k_ref[rows, cols]
            k16 = k.astype(BF16)
            kb16 = (k * b_ref[rows, cols]).astype(BF16)
            kk_qk.append(_dot_nt(jnp.concatenate([kb16, q_ref[rows, cols].astype(BF16)], axis=0),
                                 jnp.concatenate([k16, k16], axis=0)))
        for h, cols in enumerate(head_cols):
            g = g_ref[rows, cols]
            g_row = jnp.transpose(jnp.concatenate([g, g], axis=0))[0:cc, :]
            decay = jnp.where(causal, jnp.exp(jnp.where(causal, g - g_row, 0.0)), 0.0)
            neg_a.append(jnp.where(strict, -(kk_qk[h][0:cc, :] * decay), 0.0))
            at_ref[rows, h * GDN_DK:h * GDN_DK + cc] = (kk_qk[h][cc:, 0:cc] * decay[:, 0:cc]).astype(BF16)
        t_wide = _unit_lower_inverse_wide(neg_a)
        for h, cols in enumerate(head_cols):
            k = k_ref[rows, cols]
            g = g_ref[rows, cols]
            beta = b_ref[rows, cols]
            eg = jnp.exp(g)
            rhs = jnp.concatenate([(v_ref[rows, cols] * beta).astype(BF16), (k * beta * eg).astype(BF16)], axis=1)
            uw = _dot(t_wide[h][:, 0:cc].astype(BF16), rhs)
            u_ref[rows, cols] = uw[:, :GDN_DV]
            w_ref[rows, cols] = uw[:, GDN_DV:].astype(BF16)
            qd_ref[rows, cols] = (q_ref[rows, cols] * eg).astype(BF16)
            k_dec = k * jnp.exp(g[cc - 1:cc, :] - g)
            kdt_ref[ci * GDN_HEADS + h] = jnp.transpose(
                jnp.concatenate([k_dec, k_dec], axis=0))[:, 0:cc].astype(BF16)
        return carry

    def scan_chunk(ci, carry):
        rows = pl.ds(pl.multiple_of(ci * cc, cc), cc)
        tail = pl.ds(pl.multiple_of(ci * cc + cc - 8, 8), 8)
        ws_qs = [_dot(jnp.concatenate([w_ref[rows, cols], qd_ref[rows, cols]], axis=0), s_ref[h].astype(BF16))
                 for h, cols in enumerate(head_cols)]
        vn16 = [(u_ref[rows, cols] - ws_qs[h][0:cc, :]).astype(BF16) for h, cols in enumerate(head_cols)]
        for h, cols in enumerate(head_cols):
            g_last = g_ref[tail, cols][7:8, :]
            s_ref[h] = s_ref[h] * jnp.exp(g_last) + _dot(kdt_ref[ci * GDN_HEADS + h], vn16[h])
        for h, cols in enumerate(head_cols):
            o = ws_qs[h][cc:, :] + _dot(at_ref[rows, h * GDN_DK:h * GDN_DK + cc], vn16[h])
            ms = jnp.mean(o * o, axis=-1, keepdims=True)
            on = o * lax.rsqrt(ms + NORM_EPS) * nw_ref[...]
            o_ref[rows, cols] = (on * _silu(z_ref[rows, cols])).astype(o_ref.dtype)
        return carry

    lax.fori_loop(0, chunks, factor_chunk, 0)
    lax.fori_loop(0, chunks, scan_chunk, 0)


def _gdn_scan(qn, kn, vn, gcb, betab, proj, gdn_norm_w, off, b, s, chunks):
    t, w = qn.shape
    rows = chunks * GDN_CHUNK
    nblk = s // rows
    cz = off["z_a"] // w

    def tok(col):
        return pl.BlockSpec((rows, w), lambda bi, ci: (bi * nblk + ci, col))

    return pl.pallas_call(
        functools.partial(_gdn_scan_kernel, chunks=chunks),
        grid=(b, nblk),
        in_specs=[tok(0), tok(0), tok(0), tok(0), tok(0), tok(cz),
                  pl.BlockSpec((1, GDN_DV), lambda bi, ci: (0, 0))],
        out_specs=tok(0),
        out_shape=jax.ShapeDtypeStruct((t, w), BF16),
        scratch_shapes=[
            pltpu.VMEM((GDN_HEADS, GDN_DK, GDN_DV), F32),
            pltpu.VMEM((rows, w), F32),
            pltpu.VMEM((rows, w), BF16),
            pltpu.VMEM((rows, w), BF16),
            pltpu.VMEM((rows, w), BF16),
            pltpu.VMEM((chunks * GDN_HEADS, GDN_DK, GDN_CHUNK), BF16),
        ],
        compiler_params=_cparams(("arbitrary", "arbitrary")),
    )(qn, kn, vn, gcb, betab, proj, gdn_norm_w)


def _mla_prep_kernel(cq_ref, ckv_ref, kr_ref, pos_ref, qnw_ref, kvnw_ref, wq_ref, wkv_ref, invf_ref,
                     qn_ref, qr_ref, kn_ref, kro_ref, v_ref, *, qscale):
    def rms(x, w):
        ms = jnp.mean(x * x, axis=-1, keepdims=True)
        return (x * lax.rsqrt(ms + NORM_EPS) * w).astype(BF16)

    hw = MLA_HEADS * LANES
    qall = _dot(rms(cq_ref[...], qnw_ref[...]), wq_ref[...])
    kv = _dot(rms(ckv_ref[...], kvnw_ref[...]), wkv_ref[...])

    ang = pos_ref[...] * invf_ref[...]
    cos, sin = jnp.cos(ang), jnp.sin(ang)
    lane = lax.broadcasted_iota(jnp.int32, ang.shape, 1)
    half = MLA_ROPE // 2
    cm = jnp.where(lane < MLA_ROPE, cos, 0.0)
    sm = jnp.where(lane < MLA_ROPE, 0.0, jnp.where(lane < MLA_ROPE + half, -sin, sin))

    def rope(x):
        return x * cm + pltpu.roll(x * sm, MLA_ROPE, axis=1)

    qn_ref[...] = (qall[:, :hw] * qscale).astype(BF16)
    for h in range(MLA_HEADS):
        cols = slice(hw + h * LANES, hw + (h + 1) * LANES)
        qr_ref[:, h * LANES:(h + 1) * LANES] = (rope(qall[:, cols]) * qscale).astype(BF16)
    kn_ref[...] = kv[:, :hw].astype(BF16)
    for h in range(MLA_HEADS):
        v_ref[h, 0] = jnp.transpose(kv[:, hw + h * MLA_V:hw + (h + 1) * MLA_V]).astype(BF16)
    kro_ref[...] = rope(kr_ref[...]).astype(BF16)


def _mla_prep(proj, pos_col, q_norm_w, kv_norm_w, wq, wkv, invf, off, tm, qscale):
    t = proj.shape[0]
    rq, rkv = wq.shape[0], wkv.shape[0]
    hw = MLA_HEADS * LANES
    full = lambda a: pl.BlockSpec(a.shape, lambda i: (0, 0))
    big = jax.ShapeDtypeStruct((t, hw), BF16)
    return pl.pallas_call(
        functools.partial(_mla_prep_kernel, qscale=qscale),
        grid=(t // tm,),
        in_specs=[
            pl.BlockSpec((tm, rq), lambda i: (i, off["c_q"] // rq)),
            pl.BlockSpec((tm, rkv), lambda i: (i, off["c_kv"] // rkv)),
            pl.BlockSpec((tm, LANES), lambda i: (i, off["small_a"] // LANES)),
            pl.BlockSpec((tm, 1), lambda i: (i, 0)),
            full(q_norm_w), full(kv_norm_w), full(wq), full(wkv), full(invf),
        ],
        out_specs=[
            pl.BlockSpec((tm, hw), lambda i: (i, 0)),
            pl.BlockSpec((tm, hw), lambda i: (i, 0)),
            pl.BlockSpec((tm, hw), lambda i: (i, 0)),
            pl.BlockSpec((tm, LANES), lambda i: (i, 0)),
            pl.BlockSpec((MLA_HEADS, 1, MLA_V, tm), lambda i: (0, i, 0, 0)),
        ],
        out_shape=[big, big, big, jax.ShapeDtypeStruct((t, LANES), BF16),
                   jax.ShapeDtypeStruct((MLA_HEADS, t // tm, MLA_V, tm), BF16)],
        compiler_params=_cparams(("parallel",)),
    )(proj, proj, proj, pos_col, q_norm_w, kv_norm_w, wq, wkv, invf)


def _flash_kernel(qn_ref, qr_ref, kn_ref, kr_ref, vt_ref, z_ref, o_ref, s0_ref, s1_ref, p0_ref, p1_ref,
                  acc_ref, *, tk):
    tq = qn_ref.shape[0]
    assert tq == 2 * tk, "two key blocks per query tile keep the score / probability slots static"
    qi = pl.program_id(2)
    first_diag = 2 * qi
    q = jnp.concatenate([qn_ref[...], qr_ref[...]], axis=1)
    neg = -1e30
    s_ref = (s0_ref, s1_ref)
    p_ref = (p0_ref, p1_ref)

    def scores(j):
        rows = pl.ds(pl.multiple_of(j * tk, tk), tk)
        return _dot_nt(jnp.concatenate([kn_ref[rows, :], kr_ref[rows, :]], axis=1), q)

    def store_scores(slot, j):
        sc = scores(j)
        s_ref[slot][...] = sc
        return jnp.max(sc.reshape(tk // SUBLANES, SUBLANES, tq), axis=0)

    def stage(j, slot, m, l, alpha, mx, nxt, masked):
        mx_next = store_scores(1 - slot, nxt) if nxt is not None else mx
        pv = _dot(vt_ref[0, jnp.maximum(j - 1, 0)], p_ref[1 - slot][...])

        def load(r0, nrows):
            sc = s_ref[slot][r0:r0 + nrows, :]
            if masked:
                kpos = j * tk + r0 + lax.broadcasted_iota(jnp.int32, (nrows, tq), 0)
                qpos = qi * tq + lax.broadcasted_iota(jnp.int32, (nrows, tq), 1)
                sc = jnp.where(kpos <= qpos, sc, neg)
            return sc

        if masked:
            mx = load(0, SUBLANES)
            for r0 in range(SUBLANES, tk, SUBLANES):
                mx = jnp.maximum(mx, load(r0, SUBLANES))
        m_new = jnp.maximum(m, jnp.max(mx, axis=0, keepdims=True))
        alpha_new = jnp.exp2(m - m_new)
        m_rows = jnp.broadcast_to(m_new, (BF16_ROWS, tq))
        psum = jnp.zeros((SUBLANES, tq), F32)
        for r0 in range(0, tk, BF16_ROWS):
            p = jnp.exp2(load(r0, BF16_ROWS) - m_rows)
            psum = psum + p[:SUBLANES, :] + p[SUBLANES:, :]
            p_ref[slot][r0:r0 + BF16_ROWS, :] = p.astype(BF16)
        l_new = alpha_new * l + jnp.sum(psum, axis=0, keepdims=True)
        acc_ref[...] = alpha * acc_ref[...] + pv
        return m_new, l_new, alpha_new, mx_next

    def two_stages(i, carry):
        carry = stage(2 * i, 0, *carry, nxt=2 * i + 1, masked=False)
        return stage(2 * i + 1, 1, *carry, nxt=2 * i + 2, masked=False)

    acc_ref[...] = jnp.zeros_like(acc_ref)
    p1_ref[...] = jnp.zeros_like(p1_ref)
    mx0 = store_scores(0, 0)
    carry = (jnp.full((1, tq), neg, F32), jnp.zeros((1, tq), F32), jnp.ones((1, tq), F32), mx0)
    carry = lax.fori_loop(0, qi, two_stages, carry)
    carry = stage(first_diag, 0, *carry, nxt=first_diag + 1, masked=True)
    m, l, alpha, _ = stage(first_diag + 1, 1, *carry, nxt=None, masked=True)
    acc = alpha * acc_ref[...] + _dot(vt_ref[0, first_diag + 1], p1_ref[...])
    o = jnp.transpose(acc / l)
    o_ref[...] = (o * _silu(z_ref[...])).astype(o_ref.dtype)


def _flash(qn, qr, kn, kr, vt, proj, off, b, s, tq, tk):
    t = qn.shape[0]
    nq = s // tq
    nk = s // tk
    cz = off["z_b"] // LANES
    return pl.pallas_call(
        functools.partial(_flash_kernel, tk=tk),
        grid=(b, MLA_HEADS, nq),
        in_specs=[
            pl.BlockSpec((tq, LANES), lambda bi, h, i: (bi * nq + i, h)),
            pl.BlockSpec((tq, LANES), lambda bi, h, i: (bi * nq + i, h)),
            pl.BlockSpec((s, LANES), lambda bi, h, i: (bi, h)),
            pl.BlockSpec((s, LANES), lambda bi, h, i: (bi, 0)),
            pl.BlockSpec((1, nk, MLA_V, tk), lambda bi, h, i: (h, bi, 0, 0)),
            pl.BlockSpec((tq, LANES), lambda bi, h, i: (bi * nq + i, cz + h)),
        ],
        out_specs=pl.BlockSpec((tq, LANES), lambda bi, h, i: (bi * nq + i, h)),
        out_shape=jax.ShapeDtypeStruct((t, MLA_HEADS * MLA_V), BF16),
        scratch_shapes=[
            pltpu.VMEM((tk, tq), F32), pltpu.VMEM((tk, tq), F32),
            pltpu.VMEM((tk, tq), BF16), pltpu.VMEM((tk, tq), BF16),
            pltpu.VMEM((MLA_V, tq), F32),
        ],
        compiler_params=_cparams(("parallel", "parallel", "arbitrary")),
    )(qn, qr, kn, kr, vt, proj)


def _out_kernel(oa_ref, ob_ref, ga_ref, gb_ref, x_ref, pa_ref, pb_ref, wo_ref, fw_ref, y_ref):
    ya = _dot(oa_ref[...], pa_ref[...])
    yb = _dot(ob_ref[...], pb_ref[...])
    merged = _sigmoid(ga_ref[...]) * ya + _sigmoid(gb_ref[...]) * yb
    r = x_ref[...] + _dot(merged.astype(BF16), wo_ref[...])
    ms = jnp.mean(r * r, axis=-1, keepdims=True)
    y_ref[...] = r * lax.rsqrt(ms + NORM_EPS) * fw_ref[...]


def _out_stage(oa, ob, proj, x2, pa, pb, wo, fw, off, tm):
    t, d = x2.shape
    wv = oa.shape[1]
    const = lambda a: pl.BlockSpec(a.shape, lambda i: (0, 0), pipeline_mode=pl.Buffered(1))
    return pl.pallas_call(
        _out_kernel,
        grid=(t // tm,),
        in_specs=[
            pl.BlockSpec((tm, wv), lambda i: (i, 0)),
            pl.BlockSpec((tm, wv), lambda i: (i, 0)),
            pl.BlockSpec((tm, d), lambda i: (i, off["gate_a"] // d)),
            pl.BlockSpec((tm, d), lambda i: (i, off["gate_b"] // d)),
            pl.BlockSpec((tm, d), lambda i: (i, 0)),
            const(pa), const(pb), const(wo), const(fw),
        ],
        out_specs=pl.BlockSpec((tm, d), lambda i: (i, 0)),
        out_shape=jax.ShapeDtypeStruct((t, d), F32),
        compiler_params=_cparams(("parallel",)),
    )(oa, ob, proj, proj, x2, pa, pb, wo, fw)


def _layer(x2, pos_col, b, s, norm_w, w_in, conv_w, a_log, dt_bias, gdn_norm_w, q_norm_w, w_uq,
           kv_norm_w, w_ukv, proj_a, proj_b, w_out, out_norm_w):
    d = x2.shape[1]
    kw = GDN_HEADS * GDN_DK
    vw = GDN_HEADS * GDN_DV
    q_rank, kv_rank = w_uq.shape[0], w_ukv.shape[0]
    mv = MLA_HEADS * MLA_V
    widths = (2 * kw + vw, vw, GDN_HEADS, GDN_HEADS, q_rank, kv_rank, MLA_ROPE, mv, d, d)
    starts = [0]
    for wd in widths:
        starts.append(starts[-1] + wd)
    sl = lambda i: w_in[:, starts[i]:starts[i + 1]]
    w_qkv, w_za, w_alpha, w_beta, w_cq, w_ckv, w_kr, w_zb, w_ga, w_gb = (sl(i) for i in range(10))
    half = MLA_ROPE // 2
    w_kr1, w_kr2 = w_kr[:, :half], w_kr[:, half:]
    small_a = jnp.concatenate([w_kr1, w_kr2, w_kr2, w_kr1], axis=1)
    small_b = jnp.concatenate(
        [w_alpha, w_beta, jnp.zeros((d, LANES - 2 * GDN_HEADS), w_in.dtype)], axis=1)
    groups = (("gate_a", w_ga), ("gate_b", w_gb), ("qkv", w_qkv), ("z_a", w_za), ("z_b", w_zb),
              ("c_q", w_cq), ("c_kv", w_ckv), ("small_a", small_a), ("small_b", small_b))
    off, acc = {}, 0
    for name, wg in groups:
        off[name] = acc
        acc += wg.shape[1]
    w_all = jnp.concatenate([wg for _, wg in groups], axis=1).astype(BF16)

    t = x2.shape[0]
    tm1 = min(1024, t)
    tn1 = 1024 if acc % 1024 == 0 else LANES
    proj = _inproj(x2, norm_w.reshape(1, d), w_all, tm1, tn1)

    pad = lambda a: jnp.pad(a.reshape(1, -1), ((0, 0), (0, LANES - a.shape[-1])))
    dt_pad = pad(dt_bias)
    tm2 = min(256, s)
    qn, kn, vn, gcb, betab = _gdn_prep(proj, conv_w, pad(a_log), dt_pad, off, s, tm2)
    chunks = min(4, s // GDN_CHUNK)
    oa = _gdn_scan(qn, kn, vn, gcb, betab, proj, gdn_norm_w.reshape(1, GDN_DV), off, b, s, chunks)

    wq3 = w_uq.reshape(q_rank, MLA_HEADS, MLA_QK)
    wq_nope = wq3[:, :, :MLA_NOPE].reshape(q_rank, MLA_HEADS * MLA_NOPE)
    r1, r2 = wq3[:, :, MLA_NOPE:MLA_NOPE + half], wq3[:, :, MLA_NOPE + half:]
    wq_rope = jnp.concatenate([r1, r2, r2, r1], axis=2).reshape(q_rank, MLA_HEADS * LANES)
    wq = jnp.concatenate([wq_nope, wq_rope], axis=1).astype(BF16)
    wkv3 = w_ukv.reshape(kv_rank, MLA_HEADS, MLA_NOPE + MLA_V)
    wkv = jnp.concatenate([wkv3[:, :, :MLA_NOPE].reshape(kv_rank, -1),
                           wkv3[:, :, MLA_NOPE:].reshape(kv_rank, -1)], axis=1).astype(BF16)
    inv_freq = ROPE_THETA ** (-jnp.arange(0, MLA_ROPE, 2, dtype=F32) / MLA_ROPE)
    invf = jnp.tile(inv_freq, LANES // half).reshape(1, LANES)
    qscale = math.log2(math.e) / math.sqrt(MLA_QK)
    tq = min(1024, s)
    tk = tq // 2
    qnope, qrope, knope, krope, vt = _mla_prep(
        proj, pos_col, q_norm_w.reshape(1, -1), kv_norm_w.reshape(1, -1), wq, wkv, invf, off, tk, qscale)
    ob = _flash(qnope, qrope, knope, krope, vt, proj, off, b, s, tq, tk)

    tm6 = min(256, t)
    return _out_stage(oa, ob, proj, x2, proj_a.astype(BF16), proj_b.astype(BF16), w_out.astype(BF16),
                      out_norm_w.reshape(1, d), off, tm6)


def kernel(x, positions, norm_w, w_in, conv_w, a_log, dt_bias, gdn_norm_w, q_norm_w, w_uq, kv_norm_w,
           w_ukv, proj_a, proj_b, w_out, final_norm_w):
    b, s, d = x.shape
    depth = norm_w.shape[0]
    assert depth == 1, "the final norm is fused into the last layer's output stage"
    x2 = x.reshape(b * s, d)
    pos_col = positions.astype(F32).reshape(b * s, 1)
    y = _layer(x2, pos_col, b, s, norm_w[0], w_in[0], conv_w[0], a_log[0], dt_bias[0], gdn_norm_w[0],
               q_norm_w[0], w_uq[0], kv_norm_w[0], w_ukv[0], proj_a[0], proj_b[0], w_out[0], final_norm_w)
    return y.reshape(b, s, d)
```

```python
import functools
import math

import jax
import jax.numpy as jnp
from jax import lax
from jax.experimental import pallas as pl
from jax.experimental.pallas import tpu as pltpu

F32 = jnp.float32
BF16 = jnp.bfloat16
HIGHEST = lax.Precision.HIGHEST

LANES = 128
SUBLANES = 8
BF16_ROWS = 16
NORM_EPS = 1e-6
ROPE_THETA = 10000.0

GDN_HEADS = 8
GDN_DK = 128
GDN_DV = 128
GDN_CONV = 4
GDN_CHUNK = 64
MLA_HEADS = 8
MLA_NOPE = 128
MLA_ROPE = 64
MLA_V = 128
MLA_QK = MLA_NOPE + MLA_ROPE
VT_ROWS = MLA_V + BF16_ROWS

VMEM_LIMIT = 56 * 1024 * 1024


def _cparams(sem):
    return pltpu.CompilerParams(dimension_semantics=sem, vmem_limit_bytes=VMEM_LIMIT)


def _silu(x):
    return x / (1.0 + jnp.exp(-x))


def _sigmoid(x):
    return 1.0 / (1.0 + jnp.exp(-x))


def _dot(a, b):
    return jnp.dot(a, b, preferred_element_type=F32)


def _dot_nt(a, b):
    return lax.dot_general(a, b, (((1,), (1,)), ((), ())), preferred_element_type=F32)


def _dot_tn(a, b):
    return lax.dot_general(a, b, (((0,), (0,)), ((), ())), preferred_element_type=F32)


def _dot_hi(a, b):
    return jnp.dot(a, b, preferred_element_type=F32, precision=HIGHEST)


def _inproj_kernel(x_ref, nw_ref, w_ref, o_ref, h_ref):
    @pl.when(pl.program_id(1) == 0)
    def _():
        x = x_ref[...]
        ms = jnp.mean(x * x, axis=-1, keepdims=True)
        h_ref[...] = (x * lax.rsqrt(ms + NORM_EPS) * nw_ref[...]).astype(BF16)

    o_ref[...] = _dot(h_ref[...], w_ref[...])


def _inproj(x2, norm_w, w_all, tm, tn):
    t, d = x2.shape
    n = w_all.shape[1]
    return pl.pallas_call(
        _inproj_kernel,
        grid=(t // tm, n // tn),
        in_specs=[
            pl.BlockSpec((tm, d), lambda i, j: (i, 0)),
            pl.BlockSpec((1, d), lambda i, j: (0, 0)),
            pl.BlockSpec((d, tn), lambda i, j: (0, j)),
        ],
        out_specs=pl.BlockSpec((tm, tn), lambda i, j: (i, j)),
        out_shape=jax.ShapeDtypeStruct((t, n), F32),
        scratch_shapes=[pltpu.VMEM((tm, d), BF16)],
        compiler_params=_cparams(("parallel", "arbitrary")),
    )(x2, norm_w, w_all)


def _gdn_prep_kernel(q_ref, k_ref, v_ref, qh_ref, kh_ref, vh_ref, sm_ref, cw_ref, al_ref, dt_ref,
                     qo_ref, ko_ref, vo_ref, g_ref, b_ref, *, seq_tiles):
    tm = q_ref.shape[0]
    first = (pl.program_id(0) % seq_tiles) == 0
    row8 = lax.broadcasted_iota(jnp.int32, (8, 1), 0)

    def conv_silu(x_ref, halo_ref, part):
        x = x_ref[...]
        halo = jnp.where(first, 0.0, halo_ref[...])
        w = cw_ref[:, part * x.shape[1]:(part + 1) * x.shape[1]]
        acc = x * w[GDN_CONV - 1:GDN_CONV, :]
        for d in range(1, GDN_CONV):
            xs = pltpu.roll(x, d, axis=0)
            hs = pltpu.roll(halo, d, axis=0)
            head = jnp.where(row8 < d, hs, xs[0:8, :])
            xs = jnp.concatenate([head, xs[8:, :]], axis=0)
            acc = acc + xs * w[GDN_CONV - 1 - d:GDN_CONV - d, :]
        return _silu(acc)

    def l2norm_heads(y, scale):
        outs = []
        for h in range(GDN_HEADS):
            yh = y[:, h * GDN_DK:(h + 1) * GDN_DK]
            ss = jnp.sum(yh * yh, axis=-1, keepdims=True)
            outs.append(yh * (lax.rsqrt(ss + NORM_EPS) * scale))
        return jnp.concatenate(outs, axis=1)

    qo_ref[...] = l2norm_heads(conv_silu(q_ref, qh_ref, 0), GDN_DK ** -0.5)
    ko_ref[...] = l2norm_heads(conv_silu(k_ref, kh_ref, 1), 1.0)
    vo_ref[...] = conv_silu(v_ref, vh_ref, 2)

    sm = sm_ref[...]
    z = sm + dt_ref[...]
    softplus = jnp.maximum(z, 0.0) + jnp.log1p(jnp.exp(-jnp.abs(z)))
    g = -jnp.exp(al_ref[...]) * softplus
    beta = _sigmoid(sm)
    r = lax.broadcasted_iota(jnp.int32, (tm, tm), 0)
    c = lax.broadcasted_iota(jnp.int32, (tm, tm), 1)
    tri = jnp.where((r // GDN_CHUNK == c // GDN_CHUNK) & (c <= r), 1.0, 0.0).astype(F32)
    gc = _dot_hi(tri, g)
    er = lax.broadcasted_iota(jnp.int32, (LANES, GDN_HEADS * LANES), 0)
    ec = lax.broadcasted_iota(jnp.int32, (LANES, GDN_HEADS * LANES), 1)
    sel_g = jnp.where(ec // LANES == er, 1.0, 0.0).astype(F32)
    sel_b = jnp.where(ec // LANES + GDN_HEADS == er, 1.0, 0.0).astype(F32)
    g_ref[...] = _dot_hi(gc, sel_g)
    b_ref[...] = _dot_hi(beta, sel_b)


def _gdn_prep(proj, conv_w, a_log_pad, dt_pad, off, s, tm):
    t = proj.shape[0]
    w = GDN_HEADS * GDN_DK
    cq, cs = off["qkv"] // w, off["small_b"] // LANES
    hb = tm // 8

    def halo(col):
        return pl.BlockSpec((8, w), lambda i: (jnp.maximum(i * hb - 1, 0), col))

    out = jax.ShapeDtypeStruct((t, w), F32)
    return pl.pallas_call(
        functools.partial(_gdn_prep_kernel, seq_tiles=s // tm),
        grid=(t // tm,),
        in_specs=[
            pl.BlockSpec((tm, w), lambda i: (i, cq)),
            pl.BlockSpec((tm, w), lambda i: (i, cq + 1)),
            pl.BlockSpec((tm, w), lambda i: (i, cq + 2)),
            halo(cq), halo(cq + 1), halo(cq + 2),
            pl.BlockSpec((tm, LANES), lambda i: (i, cs)),
            pl.BlockSpec((GDN_CONV, 3 * w), lambda i: (0, 0)),
            pl.BlockSpec((1, LANES), lambda i: (0, 0)),
            pl.BlockSpec((1, LANES), lambda i: (0, 0)),
        ],
        out_specs=[pl.BlockSpec((tm, w), lambda i: (i, 0))] * 5,
        out_shape=[out] * 5,
        compiler_params=_cparams(("parallel",)),
    )(proj, proj, proj, proj, proj, proj, proj, conv_w, a_log_pad, dt_pad)


def _split_bf16(x):
    hi = x.astype(BF16)
    lo = (x - hi.astype(F32)).astype(BF16)
    return hi, lo


def _unit_lower_inverse_wide(neg_a_wide):
    c = neg_a_wide[0].shape[0]
    row = lax.broadcasted_iota(jnp.int32, (c, 2 * c), 0)
    lane = lax.broadcasted_iota(jnp.int32, (c, 2 * c), 1)
    left = lane < c
    eye = jnp.where(row == lane, 1.0, 0.0)
    rs = [jnp.where(left, eye, na) for na in neg_a_wide]
    zeros = jnp.zeros((c, 2 * c), BF16)
    n = 1
    while n < c:
        nxt = []
        for r in rs:
            rh, rl = _split_bf16(r)
            bh = jnp.concatenate([zeros, rh], axis=0)
            bl = jnp.concatenate([zeros, rl], axis=0)
            x = _dot(rh, jnp.concatenate([bh, bl], axis=1))
            qr = x[:, :2 * c] + x[:, 2 * c:] + _dot(rl, bh)
            nxt.append(jnp.where(left, r + qr, qr))
        rs = nxt
        n *= 2
    return rs


def _gdn_scan_kernel(q_ref, k_ref, v_ref, g_ref, b_ref, z_ref, nw_ref, o_ref,
                     s_ref, u_ref, w_ref, qd_ref, at_ref, kdt_ref, *, chunks):
    @pl.when(pl.program_id(1) == 0)
    def _():
        s_ref[...] = jnp.zeros_like(s_ref)

    cc = GDN_CHUNK
    row = lax.broadcasted_iota(jnp.int32, (cc, 2 * cc), 0)
    key = lax.broadcasted_iota(jnp.int32, (cc, 2 * cc), 1) % cc
    causal = key <= row
    strict = key < row

    heads = range(GDN_HEADS)
    head_cols = [slice(h * GDN_DK, (h + 1) * GDN_DK) for h in heads]

    def factor_chunk(ci, carry):
        rows = pl.ds(pl.multiple_of(ci * cc, cc), cc)
        kk_qk, neg_a = [], []
        for cols in head_cols:
            k = k_ref[rows, cols]
            k16 = k.astype(BF16)
            kb16 = (k * b_ref[rows, cols]).astype(BF16)
            kk_qk.append(_dot_nt(jnp.concatenate([kb16, q_ref[rows, cols].astype(BF16)], axis=0),
                                 jnp.concatenate([k16, k16], axis=0)))
        for h, cols in enumerate(head_cols):
            g = g_ref[rows, cols]
            g_row = jnp.transpose(jnp.concatenate([g, g], axis=0))[0:cc, :]
            decay = jnp.where(causal, jnp.exp(jnp.where(causal, g - g_row, 0.0)), 0.0)
            neg_a.append(jnp.where(strict, -(kk_qk[h][0:cc, :] * decay), 0.0))
            at_ref[rows, h * GDN_DK:h * GDN_DK + cc] = (kk_qk[h][cc:, 0:cc] * decay[:, 0:cc]).astype(BF16)
        t_wide = _unit_lower_inverse_wide(neg_a)
        for h, cols in enumerate(head_cols):
            k = k_ref[rows, cols]
            g = g_ref[rows, cols]
            beta = b_ref[rows, cols]
            eg = jnp.exp(g)
            rhs = jnp.concatenate([(v_ref[rows, cols] * beta).astype(BF16), (k * beta * eg).astype(BF16)], axis=1)
            uw = _dot(t_wide[h][:, 0:cc].astype(BF16), rhs)
            u_ref[rows, cols] = uw[:, :GDN_DV]
            w_ref[rows, cols] = uw[:, GDN_DV:].astype(BF16)
            qd_ref[rows, cols] = (q_ref[rows, cols] * eg).astype(BF16)
            k_dec = k * jnp.exp(g[cc - 1:cc, :] - g)
            kdt_ref[ci * GDN_HEADS + h] = jnp.transpose(
                jnp.concatenate([k_dec, k_dec], axis=0))[:, 0:cc].astype(BF16)
        return carry

    def scan_chunk(ci, carry):
        rows = pl.ds(pl.multiple_of(ci * cc, cc), cc)
        tail = pl.ds(pl.multiple_of(ci * cc + cc - 8, 8), 8)
        ws_qs = [_dot(jnp.concatenate([w_ref[rows, cols], qd_ref[rows, cols]], axis=0), s_ref[h].astype(BF16))
                 for h, cols in enumerate(head_cols)]
        vn16 = [(u_ref[rows, cols] - ws_qs[h][0:cc, :]).astype(BF16) for h, cols in enumerate(head_cols)]
        for h, cols in enumerate(head_cols):
            g_last = g_ref[tail, cols][7:8, :]
            s_ref[h] = s_ref[h] * jnp.exp(g_last) + _dot(kdt_ref[ci * GDN_HEADS + h], vn16[h])
        for h, cols in enumerate(head_cols):
            o = ws_qs[h][cc:, :] + _dot(at_ref[rows, h * GDN_DK:h * GDN_DK + cc], vn16[h])
            ms = jnp.mean(o * o, axis=-1, keepdims=True)
            on = o * lax.rsqrt(ms + NORM_EPS) * nw_ref[...]
            o_ref[rows, cols] = (on * _silu(z_ref[rows, cols])).astype(o_ref.dtype)
        return carry

    lax.fori_loop(0, chunks, factor_chunk, 0)
    lax.fori_loop(0, chunks, scan_chunk, 0)


def _gdn_scan(qn, kn, vn, gcb, betab, proj, gdn_norm_w, off, b, s, chunks):
    t, w = qn.shape
    rows = chunks * GDN_CHUNK
    nblk = s // rows
    cz = off["z_a"] // w

    def tok(col):
        return pl.BlockSpec((rows, w), lambda bi, ci: (bi * nblk + ci, col))

    return pl.pallas_call(
        functools.partial(_gdn_scan_kernel, chunks=chunks),
        grid=(b, nblk),
        in_specs=[tok(0), tok(0), tok(0), tok(0), tok(0), tok(cz),
                  pl.BlockSpec((1, GDN_DV), lambda bi, ci: (0, 0))],
        out_specs=tok(0),
        out_shape=jax.ShapeDtypeStruct((t, w), BF16),
        scratch_shapes=[
            pltpu.VMEM((GDN_HEADS, GDN_DK, GDN_DV), F32),
            pltpu.VMEM((rows, w), F32),
            pltpu.VMEM((rows, w), BF16),
            pltpu.VMEM((rows, w), BF16),
            pltpu.VMEM((rows, w), BF16),
            pltpu.VMEM((chunks * GDN_HEADS, GDN_DK, GDN_CHUNK), BF16),
        ],
        compiler_params=_cparams(("arbitrary", "arbitrary")),
    )(qn, kn, vn, gcb, betab, proj, gdn_norm_w)


def _mla_prep_kernel(cq_ref, ckv_ref, kr_ref, pos_ref, qnw_ref, kvnw_ref, wq_ref, wkv_ref, invf_ref,
                     qt_ref, kn_ref, kro_ref, v_ref, *, qscale):
    def rms(x, w):
        ms = jnp.mean(x * x, axis=-1, keepdims=True)
        return (x * lax.rsqrt(ms + NORM_EPS) * w).astype(BF16)

    hw = MLA_HEADS * LANES
    qall = _dot(rms(cq_ref[...], qnw_ref[...]), wq_ref[...])
    kv = _dot(rms(ckv_ref[...], kvnw_ref[...]), wkv_ref[...])

    ang = pos_ref[...] * invf_ref[...]
    cos, sin = jnp.cos(ang), jnp.sin(ang)
    lane = lax.broadcasted_iota(jnp.int32, ang.shape, 1)
    half = MLA_ROPE // 2
    cm = jnp.where(lane < MLA_ROPE, cos, 0.0)
    sm = jnp.where(lane < MLA_ROPE, 0.0, jnp.where(lane < MLA_ROPE + half, -sin, sin))

    def rope(x):
        return x * cm + pltpu.roll(x * sm, MLA_ROPE, axis=1)

    for h in range(MLA_HEADS):
        nope = qall[:, h * LANES:(h + 1) * LANES] * qscale
        rot = rope(qall[:, hw + h * LANES:hw + (h + 1) * LANES]) * qscale
        qt_ref[h, 0, 0:LANES, :] = jnp.transpose(nope).astype(BF16)
        qt_ref[h, 0, LANES:, :] = jnp.transpose(rot).astype(BF16)
    kn_ref[...] = kv[:, :hw].astype(BF16)
    tm = kv.shape[0]
    ones_rows = jnp.where(lax.broadcasted_iota(jnp.int32, (BF16_ROWS, tm), 0) == 0, 1.0, 0.0).astype(BF16)
    for h in range(MLA_HEADS):
        v_ref[h, 0, 0:MLA_V, :] = jnp.transpose(kv[:, hw + h * MLA_V:hw + (h + 1) * MLA_V]).astype(BF16)
        v_ref[h, 0, MLA_V:, :] = ones_rows
    kro_ref[...] = rope(kr_ref[...]).astype(BF16)


def _mla_prep(proj, pos_col, q_norm_w, kv_norm_w, wq, wkv, invf, off, tm, qscale):
    t = proj.shape[0]
    rq, rkv = wq.shape[0], wkv.shape[0]
    hw = MLA_HEADS * LANES
    full = lambda a: pl.BlockSpec(a.shape, lambda i: (0, 0))
    big = jax.ShapeDtypeStruct((t, hw), BF16)
    return pl.pallas_call(
        functools.partial(_mla_prep_kernel, qscale=qscale),
        grid=(t // tm,),
        in_specs=[
            pl.BlockSpec((tm, rq), lambda i: (i, off["c_q"] // rq)),
            pl.BlockSpec((tm, rkv), lambda i: (i, off["c_kv"] // rkv)),
            pl.BlockSpec((tm, LANES), lambda i: (i, off["small_a"] // LANES)),
            pl.BlockSpec((tm, 1), lambda i: (i, 0)),
            full(q_norm_w), full(kv_norm_w), full(wq), full(wkv), full(invf),
        ],
        out_specs=[
            pl.BlockSpec((MLA_HEADS, 1, 2 * LANES, tm), lambda i: (0, i, 0, 0)),
            pl.BlockSpec((tm, hw), lambda i: (i, 0)),
            pl.BlockSpec((tm, LANES), lambda i: (i, 0)),
            pl.BlockSpec((MLA_HEADS, 1, VT_ROWS, tm), lambda i: (0, i, 0, 0)),
        ],
        out_shape=[jax.ShapeDtypeStruct((MLA_HEADS, t // tm, 2 * LANES, tm), BF16),
                   big, jax.ShapeDtypeStruct((t, LANES), BF16),
                   jax.ShapeDtypeStruct((MLA_HEADS, t // tm, VT_ROWS, tm), BF16)],
        compiler_params=_cparams(("parallel",)),
    )(proj, proj, proj, pos_col, q_norm_w, kv_norm_w, wq, wkv, invf)


def _flash_kernel(qt_ref, kn_ref, kr_ref, vt_ref, z_ref, o_ref, s0_ref, s1_ref, p0_ref, p1_ref,
                  acc_ref, *, tk):
    tq = o_ref.shape[0]
    assert tq == 2 * tk, "two key blocks per query tile keep the score / probability slots static"
    qi = pl.program_id(2)
    first_diag = 2 * qi
    q_t = jnp.concatenate([qt_ref[0, 0], qt_ref[0, 1]], axis=1)
    neg = -1e30
    s_ref = (s0_ref, s1_ref)
    p_ref = (p0_ref, p1_ref)

    def scores(j):
        rows = pl.ds(pl.multiple_of(j * tk, tk), tk)
        return _dot(jnp.concatenate([kn_ref[rows, :], kr_ref[rows, :]], axis=1), q_t)

    def causal(d, r0, nrows):
        key = d * tk + r0 + lax.broadcasted_iota(jnp.int32, (nrows, tq), 0)
        return key <= lax.broadcasted_iota(jnp.int32, (nrows, tq), 1)

    def store_scores(slot, j, diag=None):
        sc = scores(j)
        if diag is not None:
            sc = jnp.where(causal(diag, 0, tk), sc, neg)
        s_ref[slot][...] = sc
        return jnp.max(sc.reshape(tk // SUBLANES, SUBLANES, tq), axis=0)

    def stage(j, slot, m, alpha, mx, nxt, diag=None, nxt_diag=None):
        mx_next = store_scores(1 - slot, nxt, nxt_diag) if nxt is not None else mx
        pv = _dot(vt_ref[0, jnp.maximum(j - 1, 0)], p_ref[1 - slot][...])

        def load(r0, nrows):
            sc = s_ref[slot][r0:r0 + nrows, :]
            return sc if diag is None else jnp.where(causal(diag, r0, nrows), sc, neg)

        if diag is not None:
            mx = load(0, SUBLANES)
            for r0 in range(SUBLANES, tk, SUBLANES):
                mx = jnp.maximum(mx, load(r0, SUBLANES))
        m_new = jnp.maximum(m, jnp.max(mx, axis=0, keepdims=True))
        alpha_new = jnp.exp2(m - m_new)
        m_rows = jnp.broadcast_to(m_new, (BF16_ROWS, tq))
        for r0 in range(0, tk, BF16_ROWS):
            p_ref[slot][r0:r0 + BF16_ROWS, :] = jnp.exp2(load(r0, BF16_ROWS) - m_rows).astype(BF16)
        acc_ref[...] = alpha * acc_ref[...] + pv
        return m_new, alpha_new, mx_next

    def two_stages(i, carry):
        carry = stage(2 * i, 0, *carry, nxt=2 * i + 1)
        return stage(2 * i + 1, 1, *carry, nxt=2 * i + 2)

    acc_ref[...] = jnp.zeros_like(acc_ref)
    p1_ref[...] = jnp.zeros_like(p1_ref)
    mx0 = store_scores(0, 0)
    carry = (jnp.full((1, tq), neg, F32), jnp.ones((1, tq), F32), mx0)
    carry = lax.fori_loop(0, qi, two_stages, carry)
    carry = stage(first_diag, 0, *carry, nxt=first_diag + 1, diag=0, nxt_diag=1)
    m, alpha, _ = stage(first_diag + 1, 1, *carry, nxt=None)
    acc = alpha * acc_ref[...] + _dot(vt_ref[0, first_diag + 1], p1_ref[...])
    o = jnp.transpose(acc[:MLA_V, :] / acc[MLA_V:MLA_V + 1, :])
    o_ref[...] = (o * _silu(z_ref[...])).astype(o_ref.dtype)


def _flash(qt, kn, kr, vt, proj, off, b, s, tq, tk):
    t = kn.shape[0]
    nq = s // tq
    nk = s // tk
    cz = off["z_b"] // LANES
    return pl.pallas_call(
        functools.partial(_flash_kernel, tk=tk),
        grid=(b, MLA_HEADS, nq),
        in_specs=[
            pl.BlockSpec((1, tq // tk, 2 * LANES, tk), lambda bi, h, i: (h, bi * nq + i, 0, 0)),
            pl.BlockSpec((s, LANES), lambda bi, h, i: (bi, h)),
            pl.BlockSpec((s, LANES), lambda bi, h, i: (bi, 0)),
            pl.BlockSpec((1, nk, VT_ROWS, tk), lambda bi, h, i: (h, bi, 0, 0)),
            pl.BlockSpec((tq, LANES), lambda bi, h, i: (bi * nq + i, cz + h)),
        ],
        out_specs=pl.BlockSpec((tq, LANES), lambda bi, h, i: (bi * nq + i, h)),
        out_shape=jax.ShapeDtypeStruct((t, MLA_HEADS * MLA_V), BF16),
        scratch_shapes=[
            pltpu.VMEM((tk, tq), F32), pltpu.VMEM((tk, tq), F32),
            pltpu.VMEM((tk, tq), BF16), pltpu.VMEM((tk, tq), BF16),
            pltpu.VMEM((VT_ROWS, tq), F32),
        ],
        compiler_params=_cparams(("parallel", "parallel", "arbitrary")),
    )(qt, kn, kr, vt, proj)


def _out_kernel(oa_ref, ob_ref, ga_ref, gb_ref, x_ref, pa_ref, pb_ref, wo_ref, fw_ref, y_ref):
    ya = _dot(oa_ref[...], pa_ref[...])
    yb = _dot(ob_ref[...], pb_ref[...])
    merged = _sigmoid(ga_ref[...]) * ya + _sigmoid(gb_ref[...]) * yb
    r = x_ref[...] + _dot(merged.astype(BF16), wo_ref[...])
    ms = jnp.mean(r * r, axis=-1, keepdims=True)
    y_ref[...] = r * lax.rsqrt(ms + NORM_EPS) * fw_ref[...]


def _out_stage(oa, ob, proj, x2, pa, pb, wo, fw, off, tm):
    t, d = x2.shape
    wv = oa.shape[1]
    const = lambda a: pl.BlockSpec(a.shape, lambda i: (0, 0), pipeline_mode=pl.Buffered(1))
    return pl.pallas_call(
        _out_kernel,
        grid=(t // tm,),
        in_specs=[
            pl.BlockSpec((tm, wv), lambda i: (i, 0)),
            pl.BlockSpec((tm, wv), lambda i: (i, 0)),
            pl.BlockSpec((tm, d), lambda i: (i, off["gate_a"] // d)),
            pl.BlockSpec((tm, d), lambda i: (i, off["gate_b"] // d)),
            pl.BlockSpec((tm, d), lambda i: (i, 0)),
            const(pa), const(pb), const(wo), const(fw),
        ],
        out_specs=pl.BlockSpec((tm, d), lambda i: (i, 0)),
        out_shape=jax.ShapeDtypeStruct((t, d), F32),
        compiler_params=_cparams(("parallel",)),
    )(oa, ob, proj, proj, x2, pa, pb, wo, fw)


def _layer(x2, pos_col, b, s, norm_w, w_in, conv_w, a_log, dt_bias, gdn_norm_w, q_norm_w, w_uq,
           kv_norm_w, w_ukv, proj_a, proj_b, w_out, out_norm_w):
    d = x2.shape[1]
    kw = GDN_HEADS * GDN_DK
    vw = GDN_HEADS * GDN_DV
    q_rank, kv_rank = w_uq.shape[0], w_ukv.shape[0]
    mv = MLA_HEADS * MLA_V
    widths = (2 * kw + vw, vw, GDN_HEADS, GDN_HEADS, q_rank, kv_rank, MLA_ROPE, mv, d, d)
    starts = [0]
    for wd in widths:
        starts.append(starts[-1] + wd)
    sl = lambda i: w_in[:, starts[i]:starts[i + 1]]
    w_qkv, w_za, w_alpha, w_beta, w_cq, w_ckv, w_kr, w_zb, w_ga, w_gb = (sl(i) for i in range(10))
    half = MLA_ROPE // 2
    w_kr1, w_kr2 = w_kr[:, :half], w_kr[:, half:]
    small_a = jnp.concatenate([w_kr1, w_kr2, w_kr2, w_kr1], axis=1)
    small_b = jnp.concatenate(
        [w_alpha, w_beta, jnp.zeros((d, LANES - 2 * GDN_HEADS), w_in.dtype)], axis=1)
    groups = (("gate_a", w_ga), ("gate_b", w_gb), ("qkv", w_qkv), ("z_a", w_za), ("z_b", w_zb),
              ("c_q", w_cq), ("c_kv", w_ckv), ("small_a", small_a), ("small_b", small_b))
    off, acc = {}, 0
    for name, wg in groups:
        off[name] = acc
        acc += wg.shape[1]
    w_all = jnp.concatenate([wg for _, wg in groups], axis=1).astype(BF16)

    t = x2.shape[0]
    tm1 = min(1024, t)
    tn1 = 1024 if acc % 1024 == 0 else LANES
    proj = _inproj(x2, norm_w.reshape(1, d), w_all, tm1, tn1)

    pad = lambda a: jnp.pad(a.reshape(1, -1), ((0, 0), (0, LANES - a.shape[-1])))
    dt_pad = pad(dt_bias)
    tm2 = min(256, s)
    qn, kn, vn, gcb, betab = _gdn_prep(proj, conv_w, pad(a_log), dt_pad, off, s, tm2)
    chunks = min(4, s // GDN_CHUNK)
    oa = _gdn_scan(qn, kn, vn, gcb, betab, proj, gdn_norm_w.reshape(1, GDN_DV), off, b, s, chunks)

    wq3 = w_uq.reshape(q_rank, MLA_HEADS, MLA_QK)
    wq_nope = wq3[:, :, :MLA_NOPE].reshape(q_rank, MLA_HEADS * MLA_NOPE)
    r1, r2 = wq3[:, :, MLA_NOPE:MLA_NOPE + half], wq3[:, :, MLA_NOPE + half:]
    wq_rope = jnp.concatenate([r1, r2, r2, r1], axis=2).reshape(q_rank, MLA_HEADS * LANES)
    wq = jnp.concatenate([wq_nope, wq_rope], axis=1).astype(BF16)
    wkv3 = w_ukv.reshape(kv_rank, MLA_HEADS, MLA_NOPE + MLA_V)
    wkv = jnp.concatenate([wkv3[:, :, :MLA_NOPE].reshape(kv_rank, -1),
                           wkv3[:, :, MLA_NOPE:].reshape(kv_rank, -1)], axis=1).astype(BF16)
    inv_freq = ROPE_THETA ** (-jnp.arange(0, MLA_ROPE, 2, dtype=F32) / MLA_ROPE)
    invf = jnp.tile(inv_freq, LANES // half).reshape(1, LANES)
    qscale = math.log2(math.e) / math.sqrt(MLA_QK)
    tq = min(1024, s)
    tk = tq // 2
    qt, knope, krope, vt = _mla_prep(
        proj, pos_col, q_norm_w.reshape(1, -1), kv_norm_w.reshape(1, -1), wq, wkv, invf, off, tk, qscale)
    ob = _flash(qt, knope, krope, vt, proj, off, b, s, tq, tk)

    tm6 = min(256, t)
    return _out_stage(oa, ob, proj, x2, proj_a.astype(BF16), proj_b.astype(BF16), w_out.astype(BF16),
                      out_norm_w.reshape(1, d), off, tm6)


def kernel(x, positions, norm_w, w_in, conv_w, a_log, dt_bias, gdn_norm_w, q_norm_w, w_uq, kv_norm_w,
           w_ukv, proj_a, proj_b, w_out, final_norm_w):
    b, s, d = x.shape
    depth = norm_w.shape[0]
    assert depth == 1, "the final norm is fused into the last layer's output stage"
    x2 = x.reshape(b * s, d)
    pos_col = positions.astype(F32).reshape(b * s, 1)
    y = _layer(x2, pos_col, b, s, norm_w[0], w_in[0], conv_w[0], a_log[0], dt_bias[0], gdn_norm_w[0],
               q_norm_w[0], w_uq[0], kv_norm_w[0], w_ukv[0], proj_a[0], proj_b[0], w_out[0], final_norm_w)
    return y.reshape(b, s, d)
```

```python
import functools
import math

import jax
import jax.numpy as jnp
from jax import lax
from jax.experimental import pallas as pl
from jax.experimental.pallas import tpu as pltpu

F32 = jnp.float32
BF16 = jnp.bfloat16
HIGHEST = lax.Precision.HIGHEST

LANES = 128
SUBLANES = 8
BF16_ROWS = 16
NORM_EPS = 1e-6
ROPE_THETA = 10000.0

GDN_HEADS = 8
GDN_DK = 128
GDN_DV = 128
GDN_CONV = 4
GDN_CHUNK = 64
MLA_HEADS = 8
MLA_NOPE = 128
MLA_ROPE = 64
MLA_V = 128
MLA_QK = MLA_NOPE + MLA_ROPE
VT_ROWS = MLA_V + BF16_ROWS

VMEM_LIMIT = 56 * 1024 * 1024


def _cparams(sem):
    return pltpu.CompilerParams(dimension_semantics=sem, vmem_limit_bytes=VMEM_LIMIT)


def _silu(x):
    return x / (1.0 + jnp.exp(-x))


def _sigmoid(x):
    return 1.0 / (1.0 + jnp.exp(-x))


def _dot(a, b):
    return jnp.dot(a, b, preferred_element_type=F32)


def _dot_nt(a, b):
    return lax.dot_general(a, b, (((1,), (1,)), ((), ())), preferred_element_type=F32)


def _dot_tn(a, b):
    return lax.dot_general(a, b, (((0,), (0,)), ((), ())), preferred_element_type=F32)


def _dot_hi(a, b):
    return jnp.dot(a, b, preferred_element_type=F32, precision=HIGHEST)


def _inproj_kernel(x_ref, nw_ref, w_ref, wide_ref, narrow_ref, h_ref, *, wide_tiles):
    j = pl.program_id(1)

    @pl.when(j == 0)
    def _():
        x = x_ref[...]
        ms = jnp.mean(x * x, axis=-1, keepdims=True)
        h_ref[...] = (x * lax.rsqrt(ms + NORM_EPS) * nw_ref[...]).astype(BF16)

    acc = _dot(h_ref[...], w_ref[...])

    @pl.when(j < wide_tiles)
    def _():
        wide_ref[...] = acc.astype(wide_ref.dtype)

    @pl.when(j >= wide_tiles)
    def _():
        narrow_ref[...] = acc


def _inproj(x2, norm_w, w_all, tm, tn, wide_tiles):
    t, d = x2.shape
    n = w_all.shape[1]
    assert n == (wide_tiles + 1) * tn
    return pl.pallas_call(
        functools.partial(_inproj_kernel, wide_tiles=wide_tiles),
        grid=(t // tm, n // tn),
        in_specs=[
            pl.BlockSpec((tm, d), lambda i, j: (i, 0)),
            pl.BlockSpec((1, d), lambda i, j: (0, 0)),
            pl.BlockSpec((d, tn), lambda i, j: (0, j)),
        ],
        out_specs=[pl.BlockSpec((tm, tn), lambda i, j: (i, jnp.minimum(j, wide_tiles - 1))),
                   pl.BlockSpec((tm, tn), lambda i, j: (i, 0))],
        out_shape=[jax.ShapeDtypeStruct((t, wide_tiles * tn), F32), jax.ShapeDtypeStruct((t, tn), F32)],
        scratch_shapes=[pltpu.VMEM((tm, d), BF16)],
        compiler_params=_cparams(("parallel", "arbitrary")),
    )(x2, norm_w, w_all)


def _gdn_prep_kernel(q_ref, k_ref, v_ref, qh_ref, kh_ref, vh_ref, sm_ref, cw_ref, al_ref, dt_ref,
                     qo_ref, ko_ref, vo_ref, gb_ref, xbuf_ref, *, seq_tiles):
    tm = q_ref.shape[0]
    first = (pl.program_id(0) % seq_tiles) == 0

    def conv_silu(x_ref, halo_ref, part, cols):
        x = x_ref[:, cols].astype(F32)
        xbuf_ref[0:SUBLANES, :] = jnp.where(first, 0.0, halo_ref[BF16_ROWS - SUBLANES:, cols].astype(F32))
        xbuf_ref[SUBLANES:, :] = x
        w0 = part * GDN_HEADS * GDN_DK + cols.start
        w = cw_ref[:, w0:w0 + GDN_DK]
        acc = x * w[GDN_CONV - 1:GDN_CONV, :]
        for d in range(1, GDN_CONV):
            acc = acc + xbuf_ref[SUBLANES - d:SUBLANES - d + tm, :] * w[GDN_CONV - 1 - d:GDN_CONV - d, :]
        return _silu(acc)

    def l2norm(y, scale):
        ss = jnp.sum(y * y, axis=-1, keepdims=True)
        return y * (lax.rsqrt(ss + NORM_EPS) * scale)

    sm = sm_ref[...]
    z = sm + dt_ref[...]
    softplus = jnp.maximum(z, 0.0) + jnp.log1p(jnp.exp(-jnp.abs(z)))
    g = -jnp.exp(al_ref[...]) * softplus
    beta = _sigmoid(sm)
    r = lax.broadcasted_iota(jnp.int32, (tm, tm), 0)
    c = lax.broadcasted_iota(jnp.int32, (tm, tm), 1)
    tri = jnp.where((r // GDN_CHUNK == c // GDN_CHUNK) & (c <= r), 1.0, 0.0).astype(F32)
    gc = _dot_hi(tri, g)

    for h in range(GDN_HEADS):
        cols = slice(h * GDN_DK, (h + 1) * GDN_DK)
        qo_ref[:, cols] = l2norm(conv_silu(q_ref, qh_ref, 0, cols), GDN_DK ** -0.5)
        ko_ref[:, cols] = l2norm(conv_silu(k_ref, kh_ref, 1, cols), 1.0)
        vo_ref[:, cols] = conv_silu(v_ref, vh_ref, 2, cols)
    lane = lax.broadcasted_iota(jnp.int32, gc.shape, 1)
    gb_ref[...] = jnp.where(lane < GDN_HEADS, gc, beta)


def _gdn_prep(wide, narrow, conv_w, a_log_pad, dt_pad, off, s, tm):
    t = wide.shape[0]
    w = GDN_HEADS * GDN_DK
    cq, cs = off["qkv"] // w, off["small_b"] // LANES
    hb = tm // BF16_ROWS

    def halo(col):
        return pl.BlockSpec((BF16_ROWS, w), lambda i: (jnp.maximum(i * hb - 1, 0), col))

    out = jax.ShapeDtypeStruct((t, w), F32)
    return pl.pallas_call(
        functools.partial(_gdn_prep_kernel, seq_tiles=s // tm),
        grid=(t // tm,),
        in_specs=[
            pl.BlockSpec((tm, w), lambda i: (i, cq)),
            pl.BlockSpec((tm, w), lambda i: (i, cq + 1)),
            pl.BlockSpec((tm, w), lambda i: (i, cq + 2)),
            halo(cq), halo(cq + 1), halo(cq + 2),
            pl.BlockSpec((tm, LANES), lambda i: (i, cs)),
            pl.BlockSpec((GDN_CONV, 3 * w), lambda i: (0, 0)),
            pl.BlockSpec((1, LANES), lambda i: (0, 0)),
            pl.BlockSpec((1, LANES), lambda i: (0, 0)),
        ],
        out_specs=[pl.BlockSpec((tm, w), lambda i: (i, 0))] * 3 + [pl.BlockSpec((tm, LANES), lambda i: (i, 0))],
        out_shape=[out] * 3 + [jax.ShapeDtypeStruct((t, LANES), F32)],
        scratch_shapes=[pltpu.VMEM((tm + SUBLANES, GDN_DK), F32)],
        compiler_params=_cparams(("parallel",)),
    )(wide, wide, wide, wide, wide, wide, narrow, conv_w, a_log_pad, dt_pad)


def _split_bf16(x):
    hi = x.astype(BF16)
    lo = (x - hi.astype(F32)).astype(BF16)
    return hi, lo


def _unit_lower_inverse_wide(neg_a_wide):
    c = neg_a_wide[0].shape[0]
    row = lax.broadcasted_iota(jnp.int32, (c, 2 * c), 0)
    lane = lax.broadcasted_iota(jnp.int32, (c, 2 * c), 1)
    left = lane < c
    eye = jnp.where(row == lane, 1.0, 0.0)
    rs = [jnp.where(left, eye, na) for na in neg_a_wide]
    zeros = jnp.zeros((c, 2 * c), BF16)
    n = 1
    while n < c:
        nxt = []
        for r in rs:
            rh, rl = _split_bf16(r)
            bh = jnp.concatenate([zeros, rh], axis=0)
            bl = jnp.concatenate([zeros, rl], axis=0)
            x = _dot(rh, jnp.concatenate([bh, bl], axis=1))
            qr = x[:, :2 * c] + x[:, 2 * c:] + _dot(rl, bh)
            nxt.append(jnp.where(left, r + qr, qr))
        rs = nxt
        n *= 2
    return rs


def _gdn_scan_kernel(q_ref, k_ref, v_ref, gb_ref, z_ref, nw_ref, o_ref,
                     s_ref, u_ref, w_ref, qd_ref, at_ref, kdt_ref, *, chunks):
    @pl.when(pl.program_id(1) == 0)
    def _():
        s_ref[...] = jnp.zeros_like(s_ref)

    cc = GDN_CHUNK
    row = lax.broadcasted_iota(jnp.int32, (cc, 2 * cc), 0)
    key = lax.broadcasted_iota(jnp.int32, (cc, 2 * cc), 1) % cc
    causal = key <= row
    strict = key < row

    heads = range(GDN_HEADS)
    head_cols = [slice(h * GDN_DK, (h + 1) * GDN_DK) for h in heads]

    def head_lane(rows, lane):
        return jnp.broadcast_to(gb_ref[rows, :][:, lane:lane + 1], (cc, GDN_DK))

    def factor_chunk(ci, carry):
        rows = pl.ds(pl.multiple_of(ci * cc, cc), cc)
        kk_qk, neg_a = [], []
        g_l = [head_lane(rows, h) for h in heads]
        beta_l = [head_lane(rows, GDN_HEADS + h) for h in heads]
        for h, cols in enumerate(head_cols):
            k = k_ref[rows, cols]
            k16 = k.astype(BF16)
            kb16 = (k * beta_l[h]).astype(BF16)
            kk_qk.append(_dot_nt(jnp.concatenate([kb16, q_ref[rows, cols].astype(BF16)], axis=0),
                                 jnp.concatenate([k16, k16], axis=0)))
        for h, cols in enumerate(head_cols):
            g = g_l[h]
            g_row = jnp.transpose(jnp.concatenate([g, g], axis=0))[0:cc, :]
            decay = jnp.where(causal, jnp.exp(jnp.where(causal, g - g_row, 0.0)), 0.0)
            neg_a.append(jnp.where(strict, -(kk_qk[h][0:cc, :] * decay), 0.0))
            at_ref[rows, h * GDN_DK:h * GDN_DK + cc] = (kk_qk[h][cc:, 0:cc] * decay[:, 0:cc]).astype(BF16)
        t_wide = _unit_lower_inverse_wide(neg_a)
        for h, cols in enumerate(head_cols):
            k = k_ref[rows, cols]
            g, beta = g_l[h], beta_l[h]
            eg = jnp.exp(g)
            rhs = jnp.concatenate([(v_ref[rows, cols] * beta).astype(BF16), (k * beta * eg).astype(BF16)], axis=1)
            uw = _dot(t_wide[h][:, 0:cc].astype(BF16), rhs)
            u_ref[rows, cols] = uw[:, :GDN_DV]
            w_ref[rows, cols] = uw[:, GDN_DV:].astype(BF16)
            qd_ref[rows, cols] = (q_ref[rows, cols] * eg).astype(BF16)
            k_dec = k * jnp.exp(g[cc - 1:cc, :] - g)
            kdt_ref[ci * GDN_HEADS + h] = jnp.transpose(
                jnp.concatenate([k_dec, k_dec], axis=0))[:, 0:cc].astype(BF16)
        return carry

    def scan_chunk(ci, carry):
        rows = pl.ds(pl.multiple_of(ci * cc, cc), cc)
        tail = pl.ds(pl.multiple_of(ci * cc + cc - 8, 8), 8)
        ws_qs = [_dot(jnp.concatenate([w_ref[rows, cols], qd_ref[rows, cols]], axis=0), s_ref[h].astype(BF16))
                 for h, cols in enumerate(head_cols)]
        vn16 = [(u_ref[rows, cols] - ws_qs[h][0:cc, :]).astype(BF16) for h, cols in enumerate(head_cols)]
        for h, cols in enumerate(head_cols):
            g_last = jnp.broadcast_to(gb_ref[tail, :][7:8, h:h + 1], (1, GDN_DK))
            s_ref[h] = s_ref[h] * jnp.exp(g_last) + _dot(kdt_ref[ci * GDN_HEADS + h], vn16[h])
        for h, cols in enumerate(head_cols):
            o = ws_qs[h][cc:, :] + _dot(at_ref[rows, h * GDN_DK:h * GDN_DK + cc], vn16[h])
            ms = jnp.mean(o * o, axis=-1, keepdims=True)
            on = o * lax.rsqrt(ms + NORM_EPS) * nw_ref[...]
            o_ref[rows, cols] = (on * _silu(z_ref[rows, cols].astype(F32))).astype(o_ref.dtype)
        return carry

    lax.fori_loop(0, chunks, factor_chunk, 0)
    lax.fori_loop(0, chunks, scan_chunk, 0)


def _gdn_scan(qn, kn, vn, gb, proj, gdn_norm_w, off, b, s, chunks):
    t, w = qn.shape
    rows = chunks * GDN_CHUNK
    nblk = s // rows
    cz = off["z_a"] // w

    def tok(col):
        return pl.BlockSpec((rows, w), lambda bi, ci: (bi * nblk + ci, col))

    return pl.pallas_call(
        functools.partial(_gdn_scan_kernel, chunks=chunks),
        grid=(b, nblk),
        in_specs=[tok(0), tok(0), tok(0),
                  pl.BlockSpec((rows, LANES), lambda bi, ci: (bi * nblk + ci, 0)), tok(cz),
                  pl.BlockSpec((1, GDN_DV), lambda bi, ci: (0, 0))],
        out_specs=tok(0),
        out_shape=jax.ShapeDtypeStruct((t, w), BF16),
        scratch_shapes=[
            pltpu.VMEM((GDN_HEADS, GDN_DK, GDN_DV), F32),
            pltpu.VMEM((rows, w), F32),
            pltpu.VMEM((rows, w), BF16),
            pltpu.VMEM((rows, w), BF16),
            pltpu.VMEM((rows, w), BF16),
            pltpu.VMEM((chunks * GDN_HEADS, GDN_DK, GDN_CHUNK), BF16),
        ],
        compiler_params=_cparams(("arbitrary", "arbitrary")),
    )(qn, kn, vn, gb, proj, gdn_norm_w)


def _mla_prep_kernel(cq_ref, ckv_ref, kr_ref, pos_ref, qnw_ref, kvnw_ref, wq_ref, wkv_ref, invf_ref,
                     qt_ref, kn_ref, kro_ref, v_ref, *, qscale):
    def rms(x, w):
        ms = jnp.mean(x * x, axis=-1, keepdims=True)
        return (x * lax.rsqrt(ms + NORM_EPS) * w).astype(BF16)

    hw = MLA_HEADS * LANES
    qall = _dot(rms(cq_ref[...], qnw_ref[...]), wq_ref[...])
    kv = _dot(rms(ckv_ref[...], kvnw_ref[...]), wkv_ref[...])

    ang = pos_ref[...] * invf_ref[...]
    cos, sin = jnp.cos(ang), jnp.sin(ang)
    lane = lax.broadcasted_iota(jnp.int32, ang.shape, 1)
    half = MLA_ROPE // 2
    cm = jnp.where(lane < MLA_ROPE, cos, 0.0)
    sm = jnp.where(lane < MLA_ROPE, 0.0, jnp.where(lane < MLA_ROPE + half, -sin, sin))

    def rope(x):
        return x * cm + pltpu.roll(x * sm, MLA_ROPE, axis=1)

    for h in range(MLA_HEADS):
        nope = qall[:, h * LANES:(h + 1) * LANES] * qscale
        rot = rope(qall[:, hw + h * LANES:hw + (h + 1) * LANES]) * qscale
        qt_ref[h, 0, 0:LANES, :] = jnp.transpose(nope).astype(BF16)
        qt_ref[h, 0, LANES:, :] = jnp.transpose(rot).astype(BF16)
    kn_ref[...] = kv[:, :hw].astype(BF16)
    tm = kv.shape[0]
    ones_rows = jnp.where(lax.broadcasted_iota(jnp.int32, (BF16_ROWS, tm), 0) == 0, 1.0, 0.0).astype(BF16)
    for h in range(MLA_HEADS):
        v_ref[h, 0, 0:MLA_V, :] = jnp.transpose(kv[:, hw + h * MLA_V:hw + (h + 1) * MLA_V]).astype(BF16)
        v_ref[h, 0, MLA_V:, :] = ones_rows
    kro_ref[...] = rope(kr_ref[...]).astype(BF16)


def _mla_prep(proj, pos_col, q_norm_w, kv_norm_w, wq, wkv, invf, off, tm, qscale):
    t = proj.shape[0]
    rq, rkv = wq.shape[0], wkv.shape[0]
    hw = MLA_HEADS * LANES
    full = lambda a: pl.BlockSpec(a.shape, lambda i: (0, 0))
    big = jax.ShapeDtypeStruct((t, hw), BF16)
    return pl.pallas_call(
        functools.partial(_mla_prep_kernel, qscale=qscale),
        grid=(t // tm,),
        in_specs=[
            pl.BlockSpec((tm, rq), lambda i: (i, off["c_q"] // rq)),
            pl.BlockSpec((tm, rkv), lambda i: (i, off["c_kv"] // rkv)),
            pl.BlockSpec((tm, LANES), lambda i: (i, off["small_a"] // LANES)),
            pl.BlockSpec((tm, 1), lambda i: (i, 0)),
            full(q_norm_w), full(kv_norm_w), full(wq), full(wkv), full(invf),
        ],
        out_specs=[
            pl.BlockSpec((MLA_HEADS, 1, 2 * LANES, tm), lambda i: (0, i, 0, 0)),
            pl.BlockSpec((tm, hw), lambda i: (i, 0)),
            pl.BlockSpec((tm, LANES), lambda i: (i, 0)),
            pl.BlockSpec((MLA_HEADS, 1, VT_ROWS, tm), lambda i: (0, i, 0, 0)),
        ],
        out_shape=[jax.ShapeDtypeStruct((MLA_HEADS, t // tm, 2 * LANES, tm), BF16),
                   big, jax.ShapeDtypeStruct((t, LANES), BF16),
                   jax.ShapeDtypeStruct((MLA_HEADS, t // tm, VT_ROWS, tm), BF16)],
        compiler_params=_cparams(("parallel",)),
    )(proj, proj, proj, pos_col, q_norm_w, kv_norm_w, wq, wkv, invf)


def _flash_kernel(qt_ref, kn_ref, kr_ref, vt_ref, z_ref, o_ref, s0_ref, s1_ref, p0_ref, p1_ref,
                  acc_ref, *, tk):
    tq = o_ref.shape[0]
    assert tq == 2 * tk, "two key blocks per query tile keep the score / probability slots static"
    qi = pl.program_id(2)
    first_diag = 2 * qi
    q_t = jnp.concatenate([qt_ref[0, 0], qt_ref[0, 1]], axis=1)
    neg = -1e30
    s_ref = (s0_ref, s1_ref)
    p_ref = (p0_ref, p1_ref)

    def scores(j):
        rows = pl.ds(pl.multiple_of(j * tk, tk), tk)
        return _dot(jnp.concatenate([kn_ref[rows, :], kr_ref[rows, :]], axis=1), q_t)

    def causal(d, r0, nrows):
        key = d * tk + r0 + lax.broadcasted_iota(jnp.int32, (nrows, tq), 0)
        return key <= lax.broadcasted_iota(jnp.int32, (nrows, tq), 1)

    def store_scores(slot, j, diag=None):
        sc = scores(j)
        if diag is not None:
            sc = jnp.where(causal(diag, 0, tk), sc, neg)
        s_ref[slot][...] = sc
        return jnp.max(sc.reshape(tk // SUBLANES, SUBLANES, tq), axis=0)

    def stage(j, slot, m, alpha, mx, nxt, diag=None, nxt_diag=None):
        mx_next = store_scores(1 - slot, nxt, nxt_diag) if nxt is not None else mx
        pv = _dot(vt_ref[0, jnp.maximum(j - 1, 0)], p_ref[1 - slot][...])

        def load(r0, nrows):
            sc = s_ref[slot][r0:r0 + nrows, :]
            return sc if diag is None else jnp.where(causal(diag, r0, nrows), sc, neg)

        if diag is not None:
            mx = load(0, SUBLANES)
            for r0 in range(SUBLANES, tk, SUBLANES):
                mx = jnp.maximum(mx, load(r0, SUBLANES))
        m_new = jnp.maximum(m, jnp.max(mx, axis=0, keepdims=True))
        alpha_new = jnp.exp2(m - m_new)
        m_rows = jnp.broadcast_to(m_new, (BF16_ROWS, tq))
        for r0 in range(0, tk, BF16_ROWS):
            p_ref[slot][r0:r0 + BF16_ROWS, :] = jnp.exp2(load(r0, BF16_ROWS) - m_rows).astype(BF16)
        acc_ref[...] = alpha * acc_ref[...] + pv
        return m_new, alpha_new, mx_next

    def two_stages(i, carry):
        carry = stage(2 * i, 0, *carry, nxt=2 * i + 1)
        return stage(2 * i + 1, 1, *carry, nxt=2 * i + 2)

    acc_ref[...] = jnp.zeros_like(acc_ref)
    p1_ref[...] = jnp.zeros_like(p1_ref)
    mx0 = store_scores(0, 0)
    carry = (jnp.full((1, tq), neg, F32), jnp.ones((1, tq), F32), mx0)
    carry = lax.fori_loop(0, qi, two_stages, carry)
    carry = stage(first_diag, 0, *carry, nxt=first_diag + 1, diag=0, nxt_diag=1)
    m, alpha, _ = stage(first_diag + 1, 1, *carry, nxt=None)
    acc = alpha * acc_ref[...] + _dot(vt_ref[0, first_diag + 1], p1_ref[...])
    o = jnp.transpose(acc[:MLA_V, :] / acc[MLA_V:MLA_V + 1, :])
    o_ref[...] = (o * _silu(z_ref[...].astype(F32))).astype(o_ref.dtype)


def _flash(qt, kn, kr, vt, proj, off, b, s, tq, tk):
    t = kn.shape[0]
    nq = s // tq
    nk = s // tk
    cz = off["z_b"] // LANES
    return pl.pallas_call(
        functools.partial(_flash_kernel, tk=tk),
        grid=(b, MLA_HEADS, nq),
        in_specs=[
            pl.BlockSpec((1, tq // tk, 2 * LANES, tk), lambda bi, h, i: (h, bi * nq + i, 0, 0)),
            pl.BlockSpec((s, LANES), lambda bi, h, i: (bi, h)),
            pl.BlockSpec((s, LANES), lambda bi, h, i: (bi, 0), pipeline_mode=pl.Buffered(1)),
            pl.BlockSpec((1, nk, VT_ROWS, tk), lambda bi, h, i: (h, bi, 0, 0)),
            pl.BlockSpec((tq, LANES), lambda bi, h, i: (bi * nq + i, cz + h)),
        ],
        out_specs=pl.BlockSpec((tq, LANES), lambda bi, h, i: (bi * nq + i, h)),
        out_shape=jax.ShapeDtypeStruct((t, MLA_HEADS * MLA_V), BF16),
        scratch_shapes=[
            pltpu.VMEM((tk, tq), F32), pltpu.VMEM((tk, tq), F32),
            pltpu.VMEM((tk, tq), BF16), pltpu.VMEM((tk, tq), BF16),
            pltpu.VMEM((VT_ROWS, tq), F32),
        ],
        compiler_params=_cparams(("parallel", "parallel", "arbitrary")),
    )(qt, kn, kr, vt, proj)


def _out_kernel(oa_ref, ob_ref, ga_ref, gb_ref, x_ref, pa_ref, pb_ref, wo_ref, fw_ref, y_ref):
    ya = _dot(oa_ref[...], pa_ref[...])
    yb = _dot(ob_ref[...], pb_ref[...])
    merged = _sigmoid(ga_ref[...].astype(F32)) * ya + _sigmoid(gb_ref[...].astype(F32)) * yb
    r = x_ref[...] + _dot(merged.astype(BF16), wo_ref[...])
    ms = jnp.mean(r * r, axis=-1, keepdims=True)
    y_ref[...] = r * lax.rsqrt(ms + NORM_EPS) * fw_ref[...]


def _out_stage(oa, ob, proj, x2, pa, pb, wo, fw, off, tm):
    t, d = x2.shape
    wv = oa.shape[1]
    const = lambda a: pl.BlockSpec(a.shape, lambda i: (0, 0), pipeline_mode=pl.Buffered(1))
    return pl.pallas_call(
        _out_kernel,
        grid=(t // tm,),
        in_specs=[
            pl.BlockSpec((tm, wv), lambda i: (i, 0)),
            pl.BlockSpec((tm, wv), lambda i: (i, 0)),
            pl.BlockSpec((tm, d), lambda i: (i, off["gate_a"] // d)),
            pl.BlockSpec((tm, d), lambda i: (i, off["gate_b"] // d)),
            pl.BlockSpec((tm, d), lambda i: (i, 0)),
            const(pa), const(pb), const(wo), const(fw),
        ],
        out_specs=pl.BlockSpec((tm, d), lambda i: (i, 0)),
        out_shape=jax.ShapeDtypeStruct((t, d), F32),
        compiler_params=_cparams(("parallel",)),
    )(oa, ob, proj, proj, x2, pa, pb, wo, fw)


def _layer(x2, pos_col, b, s, norm_w, w_in, conv_w, a_log, dt_bias, gdn_norm_w, q_norm_w, w_uq,
           kv_norm_w, w_ukv, proj_a, proj_b, w_out, out_norm_w):
    d = x2.shape[1]
    kw = GDN_HEADS * GDN_DK
    vw = GDN_HEADS * GDN_DV
    q_rank, kv_rank = w_uq.shape[0], w_ukv.shape[0]
    mv = MLA_HEADS * MLA_V
    widths = (2 * kw + vw, vw, GDN_HEADS, GDN_HEADS, q_rank, kv_rank, MLA_ROPE, mv, d, d)
    starts = [0]
    for wd in widths:
        starts.append(starts[-1] + wd)
    sl = lambda i: w_in[:, starts[i]:starts[i + 1]]
    w_qkv, w_za, w_alpha, w_beta, w_cq, w_ckv, w_kr, w_zb, w_ga, w_gb = (sl(i) for i in range(10))
    half = MLA_ROPE // 2
    w_kr1, w_kr2 = w_kr[:, :half], w_kr[:, half:]
    small_a = jnp.concatenate([w_kr1, w_kr2, w_kr2, w_kr1], axis=1)
    small_b = jnp.concatenate(
        [w_alpha, w_beta, jnp.zeros((d, LANES - 2 * GDN_HEADS), w_in.dtype)], axis=1)
    wide_groups = (("gate_a", w_ga), ("gate_b", w_gb), ("qkv", w_qkv), ("z_a", w_za), ("z_b", w_zb))
    narrow_groups = (("c_q", w_cq), ("c_kv", w_ckv), ("small_a", small_a), ("small_b", small_b))
    off = {}
    for groups in (wide_groups, narrow_groups):
        acc = 0
        for name, wg in groups:
            off[name] = acc
            acc += wg.shape[1]
    tn1 = acc
    wide_cols = sum(wg.shape[1] for _, wg in wide_groups)
    assert wide_cols % tn1 == 0
    w_all = jnp.concatenate([wg for _, wg in wide_groups + narrow_groups], axis=1).astype(BF16)

    t = x2.shape[0]
    tm1 = min(1024, t)
    wide, narrow = _inproj(x2, norm_w.reshape(1, d), w_all, tm1, tn1, wide_cols // tn1)

    pad = lambda a: jnp.pad(a.reshape(1, -1), ((0, 0), (0, LANES - a.shape[-1])))
    dt_pad = pad(dt_bias)
    tm2 = min(256, s)
    qn, kn, vn, gb = _gdn_prep(wide, narrow, conv_w, pad(a_log), dt_pad, off, s, tm2)
    chunks = min(4, s // GDN_CHUNK)
    oa = _gdn_scan(qn, kn, vn, gb, wide, gdn_norm_w.reshape(1, GDN_DV), off, b, s, chunks)

    wq3 = w_uq.reshape(q_rank, MLA_HEADS, MLA_QK)
    wq_nope = wq3[:, :, :MLA_NOPE].reshape(q_rank, MLA_HEADS * MLA_NOPE)
    r1, r2 = wq3[:, :, MLA_NOPE:MLA_NOPE + half], wq3[:, :, MLA_NOPE + half:]
    wq_rope = jnp.concatenate([r1, r2, r2, r1], axis=2).reshape(q_rank, MLA_HEADS * LANES)
    wq = jnp.concatenate([wq_nope, wq_rope], axis=1).astype(BF16)
    wkv3 = w_ukv.reshape(kv_rank, MLA_HEADS, MLA_NOPE + MLA_V)
    wkv = jnp.concatenate([wkv3[:, :, :MLA_NOPE].reshape(kv_rank, -1),
                           wkv3[:, :, MLA_NOPE:].reshape(kv_rank, -1)], axis=1).astype(BF16)
    inv_freq = ROPE_THETA ** (-jnp.arange(0, MLA_ROPE, 2, dtype=F32) / MLA_ROPE)
    invf = jnp.tile(inv_freq, LANES // half).reshape(1, LANES)
    qscale = math.log2(math.e) / math.sqrt(MLA_QK)
    tq = min(2048, s)
    tk = tq // 2
    qt, knope, krope, vt = _mla_prep(
        narrow, pos_col, q_norm_w.reshape(1, -1), kv_norm_w.reshape(1, -1), wq, wkv, invf, off, tk, qscale)
    ob = _flash(qt, knope, krope, vt, wide, off, b, s, tq, tk)

    tm6 = min(256, t)
    return _out_stage(oa, ob, wide, x2, proj_a.astype(BF16), proj_b.astype(BF16), w_out.astype(BF16),
                      out_norm_w.reshape(1, d), off, tm6)


def kernel(x, positions, norm_w, w_in, conv_w, a_log, dt_bias, gdn_norm_w, q_norm_w, w_uq, kv_norm_w,
           w_ukv, proj_a, proj_b, w_out, final_norm_w):
    b, s, d = x.shape
    depth = norm_w.shape[0]
    assert depth == 1, "the final norm is fused into the last layer's output stage"
    x2 = x.reshape(b * s, d)
    pos_col = positions.astype(F32).reshape(b * s, 1)
    first = lambda a: a.reshape(a.shape[1:])
    y = _layer(x2, pos_col, b, s, first(norm_w), first(w_in), first(conv_w), first(a_log), first(dt_bias),
               first(gdn_norm_w), first(q_norm_w), first(w_uq), first(kv_norm_w), first(w_ukv), first(proj_a),
               first(proj_b), first(w_out), final_norm_w)
    return y.reshape(b, s, d)
```

```python
import functools
import math

import jax
import jax.numpy as jnp
from jax import lax
from jax.experimental import pallas as pl
from jax.experimental.pallas import tpu as pltpu

F32 = jnp.float32
BF16 = jnp.bfloat16
HIGHEST = lax.Precision.HIGHEST

LANES = 128
SUBLANES = 8
BF16_ROWS = 16
NORM_EPS = 1e-6
ROPE_THETA = 10000.0

GDN_HEADS = 8
GDN_DK = 128
GDN_DV = 128
GDN_CONV = 4
GDN_CHUNK = 64
FACTOR_GROUP = 4
MLA_HEADS = 8
MLA_NOPE = 128
MLA_ROPE = 64
MLA_V = 128
MLA_QK = MLA_NOPE + MLA_ROPE
VT_ROWS = MLA_V + BF16_ROWS

VMEM_LIMIT = 58 * 1024 * 1024


def _cparams(sem):
    return pltpu.CompilerParams(dimension_semantics=sem, vmem_limit_bytes=VMEM_LIMIT)


def _silu(x):
    return x / (1.0 + jnp.exp(-x))


def _sigmoid(x):
    return 1.0 / (1.0 + jnp.exp(-x))


def _dot(a, b):
    return jnp.dot(a, b, preferred_element_type=F32)


def _dot_nt(a, b):
    return lax.dot_general(a, b, (((1,), (1,)), ((), ())), preferred_element_type=F32)


def _dot_tn(a, b):
    return lax.dot_general(a, b, (((0,), (0,)), ((), ())), preferred_element_type=F32)


def _dot_hi(a, b):
    return jnp.dot(a, b, preferred_element_type=F32, precision=HIGHEST)


def _inproj_kernel(x_ref, nw_ref, w_ref, wide_ref, narrow_ref, h_ref, *, wide_tiles):
    j = pl.program_id(1)

    @pl.when(j == 0)
    def _():
        x = x_ref[...]
        ms = jnp.mean(x * x, axis=-1, keepdims=True)
        h_ref[...] = (x * lax.rsqrt(ms + NORM_EPS) * nw_ref[...]).astype(BF16)

    acc = _dot(h_ref[...], w_ref[...])

    @pl.when(j < wide_tiles)
    def _():
        wide_ref[...] = acc.astype(wide_ref.dtype)

    @pl.when(j >= wide_tiles)
    def _():
        narrow_ref[...] = acc


def _inproj(x2, norm_w, w_all, tm, tn, wide_tiles):
    t, d = x2.shape
    n = w_all.shape[1]
    assert n == (wide_tiles + 1) * tn
    return pl.pallas_call(
        functools.partial(_inproj_kernel, wide_tiles=wide_tiles),
        grid=(t // tm, n // tn),
        in_specs=[
            pl.BlockSpec((tm, d), lambda i, j: (i, 0)),
            pl.BlockSpec((1, d), lambda i, j: (0, 0)),
            pl.BlockSpec((d, tn), lambda i, j: (0, j)),
        ],
        out_specs=[pl.BlockSpec((tm, tn), lambda i, j: (i, jnp.minimum(j, wide_tiles - 1))),
                   pl.BlockSpec((tm, tn), lambda i, j: (i, 0))],
        out_shape=[jax.ShapeDtypeStruct((t, wide_tiles * tn), F32), jax.ShapeDtypeStruct((t, tn), F32)],
        scratch_shapes=[pltpu.VMEM((tm, d), BF16)],
        compiler_params=_cparams(("parallel", "arbitrary")),
    )(x2, norm_w, w_all)


def _gdn_prep_kernel(q_ref, k_ref, v_ref, qh_ref, kh_ref, vh_ref, sm_ref, cw_ref, al_ref, dt_ref,
                     qo_ref, ko_ref, vo_ref, gb_ref, xbuf_ref, *, seq_tiles):
    tm = q_ref.shape[0]
    first = (pl.program_id(0) % seq_tiles) == 0

    def conv_silu(x_ref, halo_ref, part, cols):
        x = x_ref[:, cols].astype(F32)
        xbuf_ref[0:SUBLANES, :] = jnp.where(first, 0.0, halo_ref[BF16_ROWS - SUBLANES:, cols].astype(F32))
        xbuf_ref[SUBLANES:, :] = x
        w0 = part * GDN_HEADS * GDN_DK + cols.start
        w = cw_ref[:, w0:w0 + GDN_DK]
        acc = x * w[GDN_CONV - 1:GDN_CONV, :]
        for d in range(1, GDN_CONV):
            acc = acc + xbuf_ref[SUBLANES - d:SUBLANES - d + tm, :] * w[GDN_CONV - 1 - d:GDN_CONV - d, :]
        return _silu(acc)

    def l2norm(y, scale):
        ss = jnp.sum(y * y, axis=-1, keepdims=True)
        return y * (lax.rsqrt(ss + NORM_EPS) * scale)

    sm = sm_ref[...]
    z = sm + dt_ref[...]
    softplus = jnp.maximum(z, 0.0) + jnp.log1p(jnp.exp(-jnp.abs(z)))
    g = -jnp.exp(al_ref[...]) * softplus
    beta = _sigmoid(sm)
    r = lax.broadcasted_iota(jnp.int32, (tm, tm), 0)
    c = lax.broadcasted_iota(jnp.int32, (tm, tm), 1)
    tri = jnp.where((r // GDN_CHUNK == c // GDN_CHUNK) & (c <= r), 1.0, 0.0).astype(F32)
    gc = _dot_hi(tri, g)

    for h in range(GDN_HEADS):
        cols = slice(h * GDN_DK, (h + 1) * GDN_DK)
        qo_ref[:, cols] = l2norm(conv_silu(q_ref, qh_ref, 0, cols), GDN_DK ** -0.5)
        ko_ref[:, cols] = l2norm(conv_silu(k_ref, kh_ref, 1, cols), 1.0)
        vo_ref[:, cols] = conv_silu(v_ref, vh_ref, 2, cols)
    lane = lax.broadcasted_iota(jnp.int32, gc.shape, 1)
    gb_ref[...] = jnp.where(lane < GDN_HEADS, gc, beta)


def _gdn_prep(wide, narrow, conv_w, a_log_pad, dt_pad, off, s, tm):
    t = wide.shape[0]
    w = GDN_HEADS * GDN_DK
    cq, cs = off["qkv"] // w, off["small_b"] // LANES
    hb = tm // BF16_ROWS

    def halo(col):
        return pl.BlockSpec((BF16_ROWS, w), lambda i: (jnp.maximum(i * hb - 1, 0), col))

    out = jax.ShapeDtypeStruct((t, w), F32)
    return pl.pallas_call(
        functools.partial(_gdn_prep_kernel, seq_tiles=s // tm),
        grid=(t // tm,),
        in_specs=[
            pl.BlockSpec((tm, w), lambda i: (i, cq)),
            pl.BlockSpec((tm, w), lambda i: (i, cq + 1)),
            pl.BlockSpec((tm, w), lambda i: (i, cq + 2)),
            halo(cq), halo(cq + 1), halo(cq + 2),
            pl.BlockSpec((tm, LANES), lambda i: (i, cs)),
            pl.BlockSpec((GDN_CONV, 3 * w), lambda i: (0, 0)),
            pl.BlockSpec((1, LANES), lambda i: (0, 0)),
            pl.BlockSpec((1, LANES), lambda i: (0, 0)),
        ],
        out_specs=[pl.BlockSpec((tm, w), lambda i: (i, 0))] * 3 + [pl.BlockSpec((tm, LANES), lambda i: (i, 0))],
        out_shape=[out] * 3 + [jax.ShapeDtypeStruct((t, LANES), F32)],
        scratch_shapes=[pltpu.VMEM((tm + SUBLANES, GDN_DK), F32)],
        compiler_params=_cparams(("parallel",)),
    )(wide, wide, wide, wide, wide, wide, narrow, conv_w, a_log_pad, dt_pad)


def _split_bf16(x):
    hi = x.astype(BF16)
    lo = (x - hi.astype(F32)).astype(BF16)
    return hi, lo


def _unit_lower_inverse_wide(neg_a_wide):
    c = neg_a_wide[0].shape[0]
    row = lax.broadcasted_iota(jnp.int32, (c, 2 * c), 0)
    lane = lax.broadcasted_iota(jnp.int32, (c, 2 * c), 1)
    left = lane < c
    eye = jnp.where(row == lane, 1.0, 0.0)
    rs = [jnp.where(left, eye, na) for na in neg_a_wide]
    zeros = jnp.zeros((c, 2 * c), BF16)
    n = 1
    while n < c:
        nxt = []
        for r in rs:
            rh, rl = _split_bf16(r)
            bh = jnp.concatenate([zeros, rh], axis=0)
            bl = jnp.concatenate([zeros, rl], axis=0)
            x = _dot(rh, jnp.concatenate([bh, bl], axis=1))
            qr = x[:, :2 * c] + x[:, 2 * c:] + _dot(rl, bh)
            nxt.append(jnp.where(left, r + qr, qr))
        rs = nxt
        n *= 2
    return rs


def _gdn_scan_kernel(q_ref, k_ref, v_ref, gb_ref, z_ref, nw_ref, o_ref,
                     s_ref, u_ref, w_ref, qd_ref, at_ref, kdt_ref, *, chunks):
    @pl.when(pl.program_id(1) == 0)
    def _():
        s_ref[...] = jnp.zeros_like(s_ref)

    cc = GDN_CHUNK
    row = lax.broadcasted_iota(jnp.int32, (cc, 2 * cc), 0)
    key = lax.broadcasted_iota(jnp.int32, (cc, 2 * cc), 1) % cc
    causal = key <= row
    strict = key < row

    heads = range(GDN_HEADS)
    head_cols = [slice(h * GDN_DK, (h + 1) * GDN_DK) for h in heads]

    def head_lane(rows, lane):
        return jnp.broadcast_to(gb_ref[rows, :][:, lane:lane + 1], (cc, GDN_DK))

    def factor_chunks(gi, carry):
        units = []
        for c in range(FACTOR_GROUP):
            ci = gi * FACTOR_GROUP + c
            rows = pl.ds(pl.multiple_of(ci * cc, cc), cc)
            units += [(ci, rows, h, cols) for h, cols in enumerate(head_cols)]
        g_l = [head_lane(rows, h) for _, rows, h, _ in units]
        beta_l = [head_lane(rows, GDN_HEADS + h) for _, rows, h, _ in units]
        kk_qk, neg_a = [], []
        for n, (ci, rows, h, cols) in enumerate(units):
            k = k_ref[rows, cols]
            k16 = k.astype(BF16)
            kb16 = (k * beta_l[n]).astype(BF16)
            kk_qk.append(_dot_nt(jnp.concatenate([kb16, q_ref[rows, cols].astype(BF16)], axis=0),
                                 jnp.concatenate([k16, k16], axis=0)))
        for n, (ci, rows, h, cols) in enumerate(units):
            g = g_l[n]
            g_row = jnp.transpose(jnp.concatenate([g, g], axis=0))[0:cc, :]
            decay = jnp.where(causal, jnp.exp(jnp.where(causal, g - g_row, 0.0)), 0.0)
            neg_a.append(jnp.where(strict, -(kk_qk[n][0:cc, :] * decay), 0.0))
            at_ref[rows, h * GDN_DK:h * GDN_DK + cc] = (kk_qk[n][cc:, 0:cc] * decay[:, 0:cc]).astype(BF16)
        t_wide = _unit_lower_inverse_wide(neg_a)
        for n, (ci, rows, h, cols) in enumerate(units):
            k = k_ref[rows, cols]
            g, beta = g_l[n], beta_l[n]
            eg = jnp.exp(g)
            rhs = jnp.concatenate([(v_ref[rows, cols] * beta).astype(BF16), (k * beta * eg).astype(BF16)], axis=1)
            uw = _dot(t_wide[n][:, 0:cc].astype(BF16), rhs)
            u_ref[rows, cols] = uw[:, :GDN_DV]
            w_ref[rows, cols] = uw[:, GDN_DV:].astype(BF16)
            qd_ref[rows, cols] = (q_ref[rows, cols] * eg).astype(BF16)
            k_dec = k * jnp.exp(g[cc - 1:cc, :] - g)
            kdt_ref[ci * GDN_HEADS + h] = jnp.transpose(
                jnp.concatenate([k_dec, k_dec], axis=0))[:, 0:cc].astype(BF16)
        return carry

    def scan_chunk(ci, carry):
        rows = pl.ds(pl.multiple_of(ci * cc, cc), cc)
        tail = pl.ds(pl.multiple_of(ci * cc + cc - 8, 8), 8)
        ws_qs = [_dot(jnp.concatenate([w_ref[rows, cols], qd_ref[rows, cols]], axis=0), s_ref[h].astype(BF16))
                 for h, cols in enumerate(head_cols)]
        vn16 = [(u_ref[rows, cols] - ws_qs[h][0:cc, :]).astype(BF16) for h, cols in enumerate(head_cols)]
        for h, cols in enumerate(head_cols):
            g_last = jnp.broadcast_to(gb_ref[tail, :][7:8, h:h + 1], (1, GDN_DK))
            s_ref[h] = s_ref[h] * jnp.exp(g_last) + _dot(kdt_ref[ci * GDN_HEADS + h], vn16[h])
        for h, cols in enumerate(head_cols):
            o = ws_qs[h][cc:, :] + _dot(at_ref[rows, h * GDN_DK:h * GDN_DK + cc], vn16[h])
            ms = jnp.mean(o * o, axis=-1, keepdims=True)
            on = o * lax.rsqrt(ms + NORM_EPS) * nw_ref[...]
            o_ref[rows, cols] = (on * _silu(z_ref[rows, cols].astype(F32))).astype(o_ref.dtype)
        return carry

    assert chunks % FACTOR_GROUP == 0
    lax.fori_loop(0, chunks // FACTOR_GROUP, factor_chunks, 0)
    lax.fori_loop(0, chunks, scan_chunk, 0)


def _gdn_scan(qn, kn, vn, gb, proj, gdn_norm_w, off, b, s, chunks):
    t, w = qn.shape
    rows = chunks * GDN_CHUNK
    nblk = s // rows
    cz = off["z_a"] // w

    def tok(col):
        return pl.BlockSpec((rows, w), lambda bi, ci: (bi * nblk + ci, col))

    return pl.pallas_call(
        functools.partial(_gdn_scan_kernel, chunks=chunks),
        grid=(b, nblk),
        in_specs=[tok(0), tok(0), tok(0),
                  pl.BlockSpec((rows, LANES), lambda bi, ci: (bi * nblk + ci, 0)), tok(cz),
                  pl.BlockSpec((1, GDN_DV), lambda bi, ci: (0, 0))],
        out_specs=tok(0),
        out_shape=jax.ShapeDtypeStruct((t, w), BF16),
        scratch_shapes=[
            pltpu.VMEM((GDN_HEADS, GDN_DK, GDN_DV), F32),
            pltpu.VMEM((rows, w), F32),
            pltpu.VMEM((rows, w), BF16),
            pltpu.VMEM((rows, w), BF16),
            pltpu.VMEM((rows, w), BF16),
            pltpu.VMEM((chunks * GDN_HEADS, GDN_DK, GDN_CHUNK), BF16),
        ],
        compiler_params=_cparams(("arbitrary", "arbitrary")),
    )(qn, kn, vn, gb, proj, gdn_norm_w)


def _mla_prep_kernel(cq_ref, ckv_ref, kr_ref, pos_ref, qnw_ref, kvnw_ref, wq_ref, wkv_ref, invf_ref,
                     qt_ref, kn_ref, kro_ref, v_ref, *, qscale):
    def rms(x, w):
        ms = jnp.mean(x * x, axis=-1, keepdims=True)
        return (x * lax.rsqrt(ms + NORM_EPS) * w).astype(BF16)

    hw = MLA_HEADS * LANES
    qall = _dot(rms(cq_ref[...], qnw_ref[...]), wq_ref[...])
    kv = _dot(rms(ckv_ref[...], kvnw_ref[...]), wkv_ref[...])

    ang = pos_ref[...] * invf_ref[...]
    cos, sin = jnp.cos(ang), jnp.sin(ang)
    lane = lax.broadcasted_iota(jnp.int32, ang.shape, 1)
    half = MLA_ROPE // 2
    cm = jnp.where(lane < MLA_ROPE, cos, 0.0)
    sm = jnp.where(lane < MLA_ROPE, 0.0, jnp.where(lane < MLA_ROPE + half, -sin, sin))

    def rope(x):
        return x * cm + pltpu.roll(x * sm, MLA_ROPE, axis=1)

    for h in range(MLA_HEADS):
        nope = qall[:, h * LANES:(h + 1) * LANES] * qscale
        rot = rope(qall[:, hw + h * LANES:hw + (h + 1) * LANES]) * qscale
        qt_ref[h, 0, 0:LANES, :] = jnp.transpose(nope).astype(BF16)
        qt_ref[h, 0, LANES:, :] = jnp.transpose(rot).astype(BF16)
    kn_ref[...] = kv[:, :hw].astype(BF16)
    tm = kv.shape[0]
    ones_rows = jnp.where(lax.broadcasted_iota(jnp.int32, (BF16_ROWS, tm), 0) == 0, 1.0, 0.0).astype(BF16)
    for h in range(MLA_HEADS):
        v_ref[h, 0, 0:MLA_V, :] = jnp.transpose(kv[:, hw + h * MLA_V:hw + (h + 1) * MLA_V]).astype(BF16)
        v_ref[h, 0, MLA_V:, :] = ones_rows
    kro_ref[...] = rope(kr_ref[...]).astype(BF16)


def _mla_prep(proj, pos_col, q_norm_w, kv_norm_w, wq, wkv, invf, off, tm, qscale):
    t = proj.shape[0]
    rq, rkv = wq.shape[0], wkv.shape[0]
    hw = MLA_HEADS * LANES
    full = lambda a: pl.BlockSpec(a.shape, lambda i: (0, 0))
    big = jax.ShapeDtypeStruct((t, hw), BF16)
    return pl.pallas_call(
        functools.partial(_mla_prep_kernel, qscale=qscale),
        grid=(t // tm,),
        in_specs=[
            pl.BlockSpec((tm, rq), lambda i: (i, off["c_q"] // rq)),
            pl.BlockSpec((tm, rkv), lambda i: (i, off["c_kv"] // rkv)),
            pl.BlockSpec((tm, LANES), lambda i: (i, off["small_a"] // LANES)),
            pl.BlockSpec((tm, 1), lambda i: (i, 0)),
            full(q_norm_w), full(kv_norm_w), full(wq), full(wkv), full(invf),
        ],
        out_specs=[
            pl.BlockSpec((MLA_HEADS, 1, 2 * LANES, tm), lambda i: (0, i, 0, 0)),
            pl.BlockSpec((tm, hw), lambda i: (i, 0)),
            pl.BlockSpec((tm, LANES), lambda i: (i, 0)),
            pl.BlockSpec((MLA_HEADS, 1, VT_ROWS, tm), lambda i: (0, i, 0, 0)),
        ],
        out_shape=[jax.ShapeDtypeStruct((MLA_HEADS, t // tm, 2 * LANES, tm), BF16),
                   big, jax.ShapeDtypeStruct((t, LANES), BF16),
                   jax.ShapeDtypeStruct((MLA_HEADS, t // tm, VT_ROWS, tm), BF16)],
        compiler_params=_cparams(("parallel",)),
    )(proj, proj, proj, pos_col, q_norm_w, kv_norm_w, wq, wkv, invf)


def _flash_kernel(qt_ref, kn_ref, kr_ref, vt_ref, z_ref, o_ref, s0_ref, s1_ref, p0_ref, p1_ref,
                  acc_ref, *, tk):
    tq = o_ref.shape[0]
    assert tq == 2 * tk, "two key blocks per query tile keep the score / probability slots static"
    qi = pl.program_id(2)
    first_diag = 2 * qi
    q_t = jnp.concatenate([qt_ref[0, 0], qt_ref[0, 1]], axis=1)
    neg = -1e30
    s_ref = (s0_ref, s1_ref)
    p_ref = (p0_ref, p1_ref)

    def scores(j):
        rows = pl.ds(pl.multiple_of(j * tk, tk), tk)
        return _dot(jnp.concatenate([kn_ref[rows, :], kr_ref[rows, :]], axis=1), q_t)

    def causal(d, r0, nrows):
        key = d * tk + r0 + lax.broadcasted_iota(jnp.int32, (nrows, tq), 0)
        return key <= lax.broadcasted_iota(jnp.int32, (nrows, tq), 1)

    def store_scores(slot, j):
        sc = scores(j)
        s_ref[slot][...] = sc
        return jnp.max(sc.reshape(tk // SUBLANES, SUBLANES, tq), axis=0)

    def store_last_scores(j):
        rows = pl.ds(pl.multiple_of(j * tk, tk), tk)
        sc = _dot(jnp.concatenate([kn_ref[rows, :], kr_ref[rows, :]], axis=1), q_t[:, tk:])
        key = lax.broadcasted_iota(jnp.int32, (tk, tk), 0)
        sc = jnp.where(key <= lax.broadcasted_iota(jnp.int32, (tk, tk), 1), sc, neg)
        s1_ref[:, tk:] = sc
        return jnp.max(sc.reshape(tk // SUBLANES, SUBLANES, tk), axis=0)

    def stage(j, slot, m, alpha, mx, nxt, diag=None, nxt_last=False):
        mx_next = store_last_scores(nxt) if nxt_last else store_scores(1 - slot, nxt)
        pv = _dot(vt_ref[0, jnp.maximum(j - 1, 0)], p_ref[1 - slot][...])

        def load(r0, nrows):
            sc = s_ref[slot][r0:r0 + nrows, :]
            return sc if diag is None else jnp.where(causal(diag, r0, nrows), sc, neg)

        if diag is not None:
            mx = load(0, SUBLANES)
            for r0 in range(SUBLANES, tk, SUBLANES):
                mx = jnp.maximum(mx, load(r0, SUBLANES))
        m_new = jnp.maximum(m, jnp.max(mx, axis=0, keepdims=True))
        alpha_new = jnp.exp2(m - m_new)
        m_rows = jnp.broadcast_to(m_new, (BF16_ROWS, tq))
        for r0 in range(0, tk, BF16_ROWS):
            p_ref[slot][r0:r0 + BF16_ROWS, :] = jnp.exp2(load(r0, BF16_ROWS) - m_rows).astype(BF16)
        acc_ref[...] = alpha * acc_ref[...] + pv
        return m_new, alpha_new, mx_next

    def two_stages(i, carry):
        carry = stage(2 * i, 0, *carry, nxt=2 * i + 1)
        return stage(2 * i + 1, 1, *carry, nxt=2 * i + 2)

    acc_ref[...] = jnp.zeros_like(acc_ref)
    p1_ref[...] = jnp.zeros_like(p1_ref)
    mx0 = store_scores(0, 0)
    carry = (jnp.full((1, tq), neg, F32), jnp.ones((1, tq), F32), mx0)
    carry = lax.fori_loop(0, qi, two_stages, carry)
    m, alpha, mx_last = stage(first_diag, 0, *carry, nxt=first_diag + 1, diag=0, nxt_last=True)
    acc = alpha * acc_ref[...] + _dot(vt_ref[0, first_diag], p0_ref[...])
    m_half = m[:, tk:]
    m_new = jnp.maximum(m_half, jnp.max(mx_last, axis=0, keepdims=True))
    m_rows = jnp.broadcast_to(m_new, (BF16_ROWS, tk))
    for r0 in range(0, tk, BF16_ROWS):
        p1_ref[r0:r0 + BF16_ROWS, tk:] = jnp.exp2(s1_ref[r0:r0 + BF16_ROWS, tk:] - m_rows).astype(BF16)
    right = jnp.exp2(m_half - m_new) * acc[:, tk:] + _dot(vt_ref[0, first_diag + 1], p1_ref[:, tk:])
    acc = jnp.concatenate([acc[:, :tk], right], axis=1)
    o = jnp.transpose(acc[:MLA_V, :] / acc[MLA_V:MLA_V + 1, :])
    o_ref[...] = (o * _silu(z_ref[...].astype(F32))).astype(o_ref.dtype)


def _flash(qt, kn, kr, vt, proj, off, b, s, tq, tk):
    t = kn.shape[0]
    nq = s // tq
    nk = s // tk
    cz = off["z_b"] // LANES
    return pl.pallas_call(
        functools.partial(_flash_kernel, tk=tk),
        grid=(b, MLA_HEADS, nq),
        in_specs=[
            pl.BlockSpec((1, tq // tk, 2 * LANES, tk), lambda bi, h, i: (h, bi * nq + i, 0, 0)),
            pl.BlockSpec((s, LANES), lambda bi, h, i: (bi, h)),
            pl.BlockSpec((s, LANES), lambda bi, h, i: (bi, 0), pipeline_mode=pl.Buffered(1)),
            pl.BlockSpec((1, nk, VT_ROWS, tk), lambda bi, h, i: (h, bi, 0, 0)),
            pl.BlockSpec((tq, LANES), lambda bi, h, i: (bi * nq + i, cz + h)),
        ],
        out_specs=pl.BlockSpec((tq, LANES), lambda bi, h, i: (bi * nq + i, h)),
        out_shape=jax.ShapeDtypeStruct((t, MLA_HEADS * MLA_V), BF16),
        scratch_shapes=[
            pltpu.VMEM((tk, tq), F32), pltpu.VMEM((tk, tq), F32),
            pltpu.VMEM((tk, tq), BF16), pltpu.VMEM((tk, tq), BF16),
            pltpu.VMEM((VT_ROWS, tq), F32),
        ],
        compiler_params=_cparams(("parallel", "parallel", "arbitrary")),
    )(qt, kn, kr, vt, proj)


def _out_kernel(oa_ref, ob_ref, ga_ref, gb_ref, x_ref, pa_ref, pb_ref, wo_ref, fw_ref, y_ref):
    ya = _dot(oa_ref[...], pa_ref[...])
    yb = _dot(ob_ref[...], pb_ref[...])
    merged = _sigmoid(ga_ref[...].astype(F32)) * ya + _sigmoid(gb_ref[...].astype(F32)) * yb
    r = x_ref[...] + _dot(merged.astype(BF16), wo_ref[...])
    ms = jnp.mean(r * r, axis=-1, keepdims=True)
    y_ref[...] = r * lax.rsqrt(ms + NORM_EPS) * fw_ref[...]


def _out_stage(oa, ob, proj, x2, pa, pb, wo, fw, off, tm):
    t, d = x2.shape
    wv = oa.shape[1]
    const = lambda a: pl.BlockSpec(a.shape, lambda i: (0, 0), pipeline_mode=pl.Buffered(1))
    return pl.pallas_call(
        _out_kernel,
        grid=(t // tm,),
        in_specs=[
            pl.BlockSpec((tm, wv), lambda i: (i, 0)),
            pl.BlockSpec((tm, wv), lambda i: (i, 0)),
            pl.BlockSpec((tm, d), lambda i: (i, off["gate_a"] // d)),
            pl.BlockSpec((tm, d), lambda i: (i, off["gate_b"] // d)),
            pl.BlockSpec((tm, d), lambda i: (i, 0)),
            const(pa), const(pb), const(wo), const(fw),
        ],
        out_specs=pl.BlockSpec((tm, d), lambda i: (i, 0)),
        out_shape=jax.ShapeDtypeStruct((t, d), F32),
        compiler_params=_cparams(("parallel",)),
    )(oa, ob, proj, proj, x2, pa, pb, wo, fw)


def _layer(x2, pos_col, b, s, norm_w, w_in, conv_w, a_log, dt_bias, gdn_norm_w, q_norm_w, w_uq,
           kv_norm_w, w_ukv, proj_a, proj_b, w_out, out_norm_w):
    d = x2.shape[1]
    kw = GDN_HEADS * GDN_DK
    vw = GDN_HEADS * GDN_DV
    q_rank, kv_rank = w_uq.shape[0], w_ukv.shape[0]
    mv = MLA_HEADS * MLA_V
    widths = (2 * kw + vw, vw, GDN_HEADS, GDN_HEADS, q_rank, kv_rank, MLA_ROPE, mv, d, d)
    starts = [0]
    for wd in widths:
        starts.append(starts[-1] + wd)
    sl = lambda i: w_in[:, starts[i]:starts[i + 1]]
    w_qkv, w_za, w_alpha, w_beta, w_cq, w_ckv, w_kr, w_zb, w_ga, w_gb = (sl(i) for i in range(10))
    half = MLA_ROPE // 2
    w_kr1, w_kr2 = w_kr[:, :half], w_kr[:, half:]
    small_a = jnp.concatenate([w_kr1, w_kr2, w_kr2, w_kr1], axis=1)
    small_b = jnp.concatenate(
        [w_alpha, w_beta, jnp.zeros((d, LANES - 2 * GDN_HEADS), w_in.dtype)], axis=1)
    wide_groups = (("gate_a", w_ga), ("gate_b", w_gb), ("qkv", w_qkv), ("z_a", w_za), ("z_b", w_zb))
    narrow_groups = (("c_q", w_cq), ("c_kv", w_ckv), ("small_a", small_a), ("small_b", small_b))
    off = {}
    for groups in (wide_groups, narrow_groups):
        acc = 0
        for name, wg in groups:
            off[name] = acc
            acc += wg.shape[1]
    tn1 = acc
    wide_cols = sum(wg.shape[1] for _, wg in wide_groups)
    assert wide_cols % tn1 == 0
    w_all = jnp.concatenate([wg for _, wg in wide_groups + narrow_groups], axis=1).astype(BF16)

    t = x2.shape[0]
    tm1 = min(1024, t)
    wide, narrow = _inproj(x2, norm_w.reshape(1, d), w_all, tm1, tn1, wide_cols // tn1)

    pad = lambda a: jnp.pad(a.reshape(1, -1), ((0, 0), (0, LANES - a.shape[-1])))
    dt_pad = pad(dt_bias)
    tm2 = min(512, s)
    qn, kn, vn, gb = _gdn_prep(wide, narrow, conv_w, pad(a_log), dt_pad, off, s, tm2)
    chunks = min(4, s // GDN_CHUNK)
    oa = _gdn_scan(qn, kn, vn, gb, wide, gdn_norm_w.reshape(1, GDN_DV), off, b, s, chunks)

    wq3 = w_uq.reshape(q_rank, MLA_HEADS, MLA_QK)
    wq_nope = wq3[:, :, :MLA_NOPE].reshape(q_rank, MLA_HEADS * MLA_NOPE)
    r1, r2 = wq3[:, :, MLA_NOPE:MLA_NOPE + half], wq3[:, :, MLA_NOPE + half:]
    wq_rope = jnp.concatenate([r1, r2, r2, r1], axis=2).reshape(q_rank, MLA_HEADS * LANES)
    wq = jnp.concatenate([wq_nope, wq_rope], axis=1).astype(BF16)
    wkv3 = w_ukv.reshape(kv_rank, MLA_HEADS, MLA_NOPE + MLA_V)
    wkv = jnp.concatenate([wkv3[:, :, :MLA_NOPE].reshape(kv_rank, -1),
                           wkv3[:, :, MLA_NOPE:].reshape(kv_rank, -1)], axis=1).astype(BF16)
    inv_freq = ROPE_THETA ** (-jnp.arange(0, MLA_ROPE, 2, dtype=F32) / MLA_ROPE)
    invf = jnp.tile(inv_freq, LANES // half).reshape(1, LANES)
    qscale = math.log2(math.e) / math.sqrt(MLA_QK)
    tq = min(2048, s)
    tk = tq // 2
    qt, knope, krope, vt = _mla_prep(
        narrow, pos_col, q_norm_w.reshape(1, -1), kv_norm_w.reshape(1, -1), wq, wkv, invf, off, tk, qscale)
    ob = _flash(qt, knope, krope, vt, wide, off, b, s, tq, tk)

    tm6 = min(512, t)
    return _out_stage(oa, ob, wide, x2, proj_a.astype(BF16), proj_b.astype(BF16), w_out.astype(BF16),
                      out_norm_w.reshape(1, d), off, tm6)


def kernel(x, positions, norm_w, w_in, conv_w, a_log, dt_bias, gdn_norm_w, q_norm_w, w_uq, kv_norm_w,
           w_ukv, proj_a, proj_b, w_out, final_norm_w):
    b, s, d = x.shape
    depth = norm_w.shape[0]
    assert depth == 1, "the final norm is fused into the last layer's output stage"
    x2 = x.reshape(b * s, d)
    pos_col = positions.astype(F32).reshape(b * s, 1)
    first = lambda a: a.reshape(a.shape[1:])
    y = _layer(x2, pos_col, b, s, first(norm_w), first(w_in), first(conv_w), first(a_log), first(dt_bias),
               first(gdn_norm_w), first(q_norm_w), first(w_uq), first(kv_norm_w), first(w_ukv), first(proj_a),
               first(proj_b), first(w_out), final_norm_w)
    return y.reshape(b, s, d)
```

```python
import functools
import math

import jax
import jax.numpy as jnp
from jax import lax
from jax.experimental import pallas as pl
from jax.experimental.pallas import tpu as pltpu

F32 = jnp.float32
BF16 = jnp.bfloat16
HIGHEST = lax.Precision.HIGHEST

LANES = 128
SUBLANES = 8
BF16_ROWS = 16
NORM_EPS = 1e-6
ROPE_THETA = 10000.0

GDN_HEADS = 8
GDN_DK = 128
GDN_DV = 128
GDN_CONV = 4
GDN_CHUNK = 64
FACTOR_GROUP = 4
MLA_HEADS = 8
MLA_NOPE = 128
MLA_ROPE = 64
MLA_V = 128
MLA_QK = MLA_NOPE + MLA_ROPE
VT_ROWS = MLA_V + BF16_ROWS

VMEM_LIMIT = 58 * 1024 * 1024


def _tile_sizes(t, s):
    flash_q = min(2048, s)
    return dict(
        inproj_rows=min(1024, t),
        prep_rows=min(512, s),
        scan_chunks=min(FACTOR_GROUP, s // GDN_CHUNK),
        flash_q=flash_q,
        flash_k=flash_q // 2,
        out_rows=min(512, t),
    )


def _cparams(sem):
    return pltpu.CompilerParams(dimension_semantics=sem, vmem_limit_bytes=VMEM_LIMIT)


def _silu(x):
    return x / (1.0 + jnp.exp(-x))


def _sigmoid(x):
    return 1.0 / (1.0 + jnp.exp(-x))


def _dot(a, b):
    return jnp.dot(a, b, preferred_element_type=F32)


def _dot_nt(a, b):
    return lax.dot_general(a, b, (((1,), (1,)), ((), ())), preferred_element_type=F32)


def _dot_hi(a, b):
    return jnp.dot(a, b, preferred_element_type=F32, precision=HIGHEST)


def _inproj_kernel(x_ref, nw_ref, w_ref, wide_ref, narrow_ref, h_ref, *, wide_tiles):
    j = pl.program_id(1)

    @pl.when(j == 0)
    def _():
        x = x_ref[...]
        ms = jnp.mean(x * x, axis=-1, keepdims=True)
        h_ref[...] = (x * lax.rsqrt(ms + NORM_EPS) * nw_ref[...]).astype(BF16)

    acc = _dot(h_ref[...], w_ref[...])

    @pl.when(j < wide_tiles)
    def _():
        wide_ref[...] = acc.astype(wide_ref.dtype)

    @pl.when(j >= wide_tiles)
    def _():
        narrow_ref[...] = acc


def _inproj(x2, norm_w, w_all, tm, tn, wide_tiles):
    t, d = x2.shape
    n = w_all.shape[1]
    assert n == (wide_tiles + 1) * tn
    return pl.pallas_call(
        functools.partial(_inproj_kernel, wide_tiles=wide_tiles),
        grid=(t // tm, n // tn),
        in_specs=[
            pl.BlockSpec((tm, d), lambda i, j: (i, 0)),
            pl.BlockSpec((1, d), lambda i, j: (0, 0)),
            pl.BlockSpec((d, tn), lambda i, j: (0, j)),
        ],
        out_specs=[pl.BlockSpec((tm, tn), lambda i, j: (i, jnp.minimum(j, wide_tiles - 1))),
                   pl.BlockSpec((tm, tn), lambda i, j: (i, 0))],
        out_shape=[jax.ShapeDtypeStruct((t, wide_tiles * tn), F32), jax.ShapeDtypeStruct((t, tn), F32)],
        scratch_shapes=[pltpu.VMEM((tm, d), BF16)],
        compiler_params=_cparams(("parallel", "arbitrary")),
    )(x2, norm_w, w_all)


def _gdn_prep_kernel(q_ref, k_ref, v_ref, qh_ref, kh_ref, vh_ref, sm_ref, cw_ref, al_ref, dt_ref,
                     qo_ref, ko_ref, vo_ref, gb_ref, xbuf_ref, *, seq_tiles):
    tm = q_ref.shape[0]
    first = (pl.program_id(0) % seq_tiles) == 0

    def conv_silu(x_ref, halo_ref, part, cols):
        x = x_ref[:, cols].astype(F32)
        xbuf_ref[0:SUBLANES, :] = jnp.where(first, 0.0, halo_ref[BF16_ROWS - SUBLANES:, cols].astype(F32))
        xbuf_ref[SUBLANES:, :] = x
        w0 = part * GDN_HEADS * GDN_DK + cols.start
        w = cw_ref[:, w0:w0 + GDN_DK]
        acc = x * w[GDN_CONV - 1:GDN_CONV, :]
        for d in range(1, GDN_CONV):
            acc = acc + xbuf_ref[SUBLANES - d:SUBLANES - d + tm, :] * w[GDN_CONV - 1 - d:GDN_CONV - d, :]
        return _silu(acc)

    def l2norm(y, scale):
        ss = jnp.sum(y * y, axis=-1, keepdims=True)
        return y * (lax.rsqrt(ss + NORM_EPS) * scale)

    sm = sm_ref[...]
    z = sm + dt_ref[...]
    softplus = jnp.maximum(z, 0.0) + jnp.log1p(jnp.exp(-jnp.abs(z)))
    g = -jnp.exp(al_ref[...]) * softplus
    beta = _sigmoid(sm)
    r = lax.broadcasted_iota(jnp.int32, (tm, tm), 0)
    c = lax.broadcasted_iota(jnp.int32, (tm, tm), 1)
    tri = jnp.where((r // GDN_CHUNK == c // GDN_CHUNK) & (c <= r), 1.0, 0.0).astype(F32)
    gc = _dot_hi(tri, g)

    for h in range(GDN_HEADS):
        cols = slice(h * GDN_DK, (h + 1) * GDN_DK)
        qo_ref[:, cols] = l2norm(conv_silu(q_ref, qh_ref, 0, cols), GDN_DK ** -0.5)
        ko_ref[:, cols] = l2norm(conv_silu(k_ref, kh_ref, 1, cols), 1.0)
        vo_ref[:, cols] = conv_silu(v_ref, vh_ref, 2, cols)
    lane = lax.broadcasted_iota(jnp.int32, gc.shape, 1)
    gb_ref[...] = jnp.where(lane < GDN_HEADS, gc, beta)


def _gdn_prep(wide, narrow, conv_w, a_log_pad, dt_pad, off, s, tm):
    t = wide.shape[0]
    w = GDN_HEADS * GDN_DK
    cq, cs = off["qkv"] // w, off["small_b"] // LANES
    hb = tm // BF16_ROWS

    def halo(col):
        return pl.BlockSpec((BF16_ROWS, w), lambda i: (jnp.maximum(i * hb - 1, 0), col))

    out = jax.ShapeDtypeStruct((t, w), F32)
    return pl.pallas_call(
        functools.partial(_gdn_prep_kernel, seq_tiles=s // tm),
        grid=(t // tm,),
        in_specs=[
            pl.BlockSpec((tm, w), lambda i: (i, cq)),
            pl.BlockSpec((tm, w), lambda i: (i, cq + 1)),
            pl.BlockSpec((tm, w), lambda i: (i, cq + 2)),
            halo(cq), halo(cq + 1), halo(cq + 2),
            pl.BlockSpec((tm, LANES), lambda i: (i, cs)),
            pl.BlockSpec((GDN_CONV, 3 * w), lambda i: (0, 0)),
            pl.BlockSpec((1, LANES), lambda i: (0, 0)),
            pl.BlockSpec((1, LANES), lambda i: (0, 0)),
        ],
        out_specs=[pl.BlockSpec((tm, w), lambda i: (i, 0))] * 3 + [pl.BlockSpec((tm, LANES), lambda i: (i, 0))],
        out_shape=[out] * 3 + [jax.ShapeDtypeStruct((t, LANES), F32)],
        scratch_shapes=[pltpu.VMEM((tm + SUBLANES, GDN_DK), F32)],
        compiler_params=_cparams(("parallel",)),
    )(wide, wide, wide, wide, wide, wide, narrow, conv_w, a_log_pad, dt_pad)


def _split_bf16(x):
    hi = x.astype(BF16)
    lo = (x - hi.astype(F32)).astype(BF16)
    return hi, lo


def _unit_lower_inverse_wide(neg_a_wide):
    c = neg_a_wide[0].shape[0]
    row = lax.broadcasted_iota(jnp.int32, (c, 2 * c), 0)
    lane = lax.broadcasted_iota(jnp.int32, (c, 2 * c), 1)
    left = lane < c
    eye = jnp.where(row == lane, 1.0, 0.0)
    rs = [jnp.where(left, eye, na) for na in neg_a_wide]
    zeros = jnp.zeros((c, 2 * c), BF16)
    n = 1
    while n < c:
        nxt = []
        for r in rs:
            rh, rl = _split_bf16(r)
            bh = jnp.concatenate([zeros, rh], axis=0)
            bl = jnp.concatenate([zeros, rl], axis=0)
            x = _dot(rh, jnp.concatenate([bh, bl], axis=1))
            qr = x[:, :2 * c] + x[:, 2 * c:] + _dot(rl, bh)
            nxt.append(jnp.where(left, r + qr, qr))
        rs = nxt
        n *= 2
    return rs


def _gdn_scan_kernel(q_ref, k_ref, v_ref, gb_ref, z_ref, nw_ref, o_ref,
                     s_ref, u_ref, w_ref, qd_ref, at_ref, kdt_ref, *, chunks):
    @pl.when(pl.program_id(1) == 0)
    def _():
        s_ref[...] = jnp.zeros_like(s_ref)

    cc = GDN_CHUNK
    row = lax.broadcasted_iota(jnp.int32, (cc, 2 * cc), 0)
    key = lax.broadcasted_iota(jnp.int32, (cc, 2 * cc), 1) % cc
    causal = key <= row
    strict = key < row

    heads = range(GDN_HEADS)
    head_cols = [slice(h * GDN_DK, (h + 1) * GDN_DK) for h in heads]

    def head_lane(rows, lane):
        return jnp.broadcast_to(gb_ref[rows, :][:, lane:lane + 1], (cc, GDN_DK))

    def factor_chunks(gi, carry):
        units = []
        for c in range(FACTOR_GROUP):
            ci = gi * FACTOR_GROUP + c
            rows = pl.ds(pl.multiple_of(ci * cc, cc), cc)
            units += [(ci, rows, h, cols) for h, cols in enumerate(head_cols)]
        g_l = [head_lane(rows, h) for _, rows, h, _ in units]
        beta_l = [head_lane(rows, GDN_HEADS + h) for _, rows, h, _ in units]
        kk_qk, neg_a = [], []
        for n, (ci, rows, h, cols) in enumerate(units):
            k = k_ref[rows, cols]
            k16 = k.astype(BF16)
            kb16 = (k * beta_l[n]).astype(BF16)
            kk_qk.append(_dot_nt(jnp.concatenate([kb16, q_ref[rows, cols].astype(BF16)], axis=0),
                                 jnp.concatenate([k16, k16], axis=0)))
        for n, (ci, rows, h, cols) in enumerate(units):
            g = g_l[n]
            g_row = jnp.transpose(jnp.concatenate([g, g], axis=0))[0:cc, :]
            decay = jnp.where(causal, jnp.exp(jnp.where(causal, g - g_row, 0.0)), 0.0)
            neg_a.append(jnp.where(strict, -(kk_qk[n][0:cc, :] * decay), 0.0))
            at_ref[rows, h * GDN_DK:h * GDN_DK + cc] = (kk_qk[n][cc:, 0:cc] * decay[:, 0:cc]).astype(BF16)
        t_wide = _unit_lower_inverse_wide(neg_a)
        for n, (ci, rows, h, cols) in enumerate(units):
            k = k_ref[rows, cols]
            g, beta = g_l[n], beta_l[n]
            eg = jnp.exp(g)
            rhs = jnp.concatenate([(v_ref[rows, cols] * beta).astype(BF16), (k * beta * eg).astype(BF16)], axis=1)
            uw = _dot(t_wide[n][:, 0:cc].astype(BF16), rhs)
            u_ref[rows, cols] = uw[:, :GDN_DV]
            w_ref[rows, cols] = uw[:, GDN_DV:].astype(BF16)
            qd_ref[rows, cols] = (q_ref[rows, cols] * eg).astype(BF16)
            k_dec = k * jnp.exp(g[cc - 1:cc, :] - g)
            kdt_ref[ci * GDN_HEADS + h] = jnp.transpose(
                jnp.concatenate([k_dec, k_dec], axis=0))[:, 0:cc].astype(BF16)
        return carry

    def scan_chunk(ci, carry):
        rows = pl.ds(pl.multiple_of(ci * cc, cc), cc)
        tail = pl.ds(pl.multiple_of(ci * cc + cc - 8, 8), 8)
        ws_qs = [_dot(jnp.concatenate([w_ref[rows, cols], qd_ref[rows, cols]], axis=0), s_ref[h].astype(BF16))
                 for h, cols in enumerate(head_cols)]
        vn16 = [(u_ref[rows, cols] - ws_qs[h][0:cc, :]).astype(BF16) for h, cols in enumerate(head_cols)]
        for h, cols in enumerate(head_cols):
            g_last = jnp.broadcast_to(gb_ref[tail, :][7:8, h:h + 1], (1, GDN_DK))
            s_ref[h] = s_ref[h] * jnp.exp(g_last) + _dot(kdt_ref[ci * GDN_HEADS + h], vn16[h])
        for h, cols in enumerate(head_cols):
            o = ws_qs[h][cc:, :] + _dot(at_ref[rows, h * GDN_DK:h * GDN_DK + cc], vn16[h])
            ms = jnp.mean(o * o, axis=-1, keepdims=True)
            on = o * lax.rsqrt(ms + NORM_EPS) * nw_ref[...]
            o_ref[rows, cols] = (on * _silu(z_ref[rows, cols].astype(F32))).astype(o_ref.dtype)
        return carry

    assert chunks % FACTOR_GROUP == 0
    lax.fori_loop(0, chunks // FACTOR_GROUP, factor_chunks, 0)
    lax.fori_loop(0, chunks, scan_chunk, 0)


def _gdn_scan(qn, kn, vn, gb, proj, gdn_norm_w, off, b, s, chunks):
    t, w = qn.shape
    rows = chunks * GDN_CHUNK
    nblk = s // rows
    cz = off["z_a"] // w

    def tok(col):
        return pl.BlockSpec((rows, w), lambda bi, ci: (bi * nblk + ci, col))

    return pl.pallas_call(
        functools.partial(_gdn_scan_kernel, chunks=chunks),
        grid=(b, nblk),
        in_specs=[tok(0), tok(0), tok(0),
                  pl.BlockSpec((rows, LANES), lambda bi, ci: (bi * nblk + ci, 0)), tok(cz),
                  pl.BlockSpec((1, GDN_DV), lambda bi, ci: (0, 0))],
        out_specs=tok(0),
        out_shape=jax.ShapeDtypeStruct((t, w), BF16),
        scratch_shapes=[
            pltpu.VMEM((GDN_HEADS, GDN_DK, GDN_DV), F32),
            pltpu.VMEM((rows, w), F32),
            pltpu.VMEM((rows, w), BF16),
            pltpu.VMEM((rows, w), BF16),
            pltpu.VMEM((rows, w), BF16),
            pltpu.VMEM((chunks * GDN_HEADS, GDN_DK, GDN_CHUNK), BF16),
        ],
        compiler_params=_cparams(("arbitrary", "arbitrary")),
    )(qn, kn, vn, gb, proj, gdn_norm_w)


def _mla_prep_kernel(cq_ref, ckv_ref, kr_ref, pos_ref, qnw_ref, kvnw_ref, wq_ref, wkv_ref, invf_ref,
                     qt_ref, kn_ref, kro_ref, v_ref, *, qscale):
    def rms(x, w):
        ms = jnp.mean(x * x, axis=-1, keepdims=True)
        return (x * lax.rsqrt(ms + NORM_EPS) * w).astype(BF16)

    hw = MLA_HEADS * LANES
    qall = _dot(rms(cq_ref[...], qnw_ref[...]), wq_ref[...])
    kv = _dot(rms(ckv_ref[...], kvnw_ref[...]), wkv_ref[...])

    ang = pos_ref[...] * invf_ref[...]
    cos, sin = jnp.cos(ang), jnp.sin(ang)
    lane = lax.broadcasted_iota(jnp.int32, ang.shape, 1)
    half = MLA_ROPE // 2
    cm = jnp.where(lane < MLA_ROPE, cos, 0.0)
    sm = jnp.where(lane < MLA_ROPE, 0.0, jnp.where(lane < MLA_ROPE + half, -sin, sin))

    def rope(x):
        return x * cm + pltpu.roll(x * sm, MLA_ROPE, axis=1)

    for h in range(MLA_HEADS):
        nope = qall[:, h * LANES:(h + 1) * LANES] * qscale
        rot = rope(qall[:, hw + h * LANES:hw + (h + 1) * LANES]) * qscale
        qt_ref[h, 0, 0:LANES, :] = jnp.transpose(nope).astype(BF16)
        qt_ref[h, 0, LANES:, :] = jnp.transpose(rot).astype(BF16)
    kn_ref[...] = kv[:, :hw].astype(BF16)
    tm = kv.shape[0]
    ones_rows = jnp.where(lax.broadcasted_iota(jnp.int32, (BF16_ROWS, tm), 0) == 0, 1.0, 0.0).astype(BF16)
    for h in range(MLA_HEADS):
        v_ref[h, 0, 0:MLA_V, :] = jnp.transpose(kv[:, hw + h * MLA_V:hw + (h + 1) * MLA_V]).astype(BF16)
        v_ref[h, 0, MLA_V:, :] = ones_rows
    kro_ref[...] = rope(kr_ref[...]).astype(BF16)


def _mla_prep(proj, pos_col, q_norm_w, kv_norm_w, wq, wkv, invf, off, tm, qscale):
    t = proj.shape[0]
    rq, rkv = wq.shape[0], wkv.shape[0]
    hw = MLA_HEADS * LANES
    full = lambda a: pl.BlockSpec(a.shape, lambda i: (0, 0))
    big = jax.ShapeDtypeStruct((t, hw), BF16)
    return pl.pallas_call(
        functools.partial(_mla_prep_kernel, qscale=qscale),
        grid=(t // tm,),
        in_specs=[
            pl.BlockSpec((tm, rq), lambda i: (i, off["c_q"] // rq)),
            pl.BlockSpec((tm, rkv), lambda i: (i, off["c_kv"] // rkv)),
            pl.BlockSpec((tm, LANES), lambda i: (i, off["small_a"] // LANES)),
            pl.BlockSpec((tm, 1), lambda i: (i, 0)),
            full(q_norm_w), full(kv_norm_w), full(wq), full(wkv), full(invf),
        ],
        out_specs=[
            pl.BlockSpec((MLA_HEADS, 1, 2 * LANES, tm), lambda i: (0, i, 0, 0)),
            pl.BlockSpec((tm, hw), lambda i: (i, 0)),
            pl.BlockSpec((tm, LANES), lambda i: (i, 0)),
            pl.BlockSpec((MLA_HEADS, 1, VT_ROWS, tm), lambda i: (0, i, 0, 0)),
        ],
        out_shape=[jax.ShapeDtypeStruct((MLA_HEADS, t // tm, 2 * LANES, tm), BF16),
                   big, jax.ShapeDtypeStruct((t, LANES), BF16),
                   jax.ShapeDtypeStruct((MLA_HEADS, t // tm, VT_ROWS, tm), BF16)],
        compiler_params=_cparams(("parallel",)),
    )(proj, proj, proj, pos_col, q_norm_w, kv_norm_w, wq, wkv, invf)


def _flash_kernel(qt_ref, qtn_ref, kn_ref, kr_ref, vt_ref, z_ref, o_ref, s0_ref, s1_ref, p0_ref, p1_ref,
                  acc_ref, mx0_ref, *, tk):
    tq = o_ref.shape[0]
    assert tq == 2 * tk, "two key blocks per query tile keep the score / probability slots static"
    qi = pl.program_id(2)
    first_diag = 2 * qi
    q_t = jnp.concatenate([qt_ref[0, 0], qt_ref[0, 1]], axis=1)
    neg = -1e30
    s_ref = (s0_ref, s1_ref)
    p_ref = (p0_ref, p1_ref)

    def scores(j):
        rows = pl.ds(pl.multiple_of(j * tk, tk), tk)
        return _dot(jnp.concatenate([kn_ref[rows, :], kr_ref[rows, :]], axis=1), q_t)

    def causal(d, r0, nrows):
        key = d * tk + r0 + lax.broadcasted_iota(jnp.int32, (nrows, tq), 0)
        return key <= lax.broadcasted_iota(jnp.int32, (nrows, tq), 1)

    def store_scores(slot, j):
        sc = scores(j)
        s_ref[slot][...] = sc
        return jnp.max(sc.reshape(tk // SUBLANES, SUBLANES, tq), axis=0)

    def store_last_scores(j):
        rows = pl.ds(pl.multiple_of(j * tk, tk), tk)
        sc = _dot(jnp.concatenate([kn_ref[rows, :], kr_ref[rows, :]], axis=1), q_t[:, tk:])
        key = lax.broadcasted_iota(jnp.int32, (tk, tk), 0)
        sc = jnp.where(key <= lax.broadcasted_iota(jnp.int32, (tk, tk), 1), sc, neg)
        s1_ref[:, tk:] = sc
        return jnp.max(sc.reshape(tk // SUBLANES, SUBLANES, tk), axis=0)

    def stage(j, slot, m, alpha, mx, nxt, diag=None, nxt_last=False):
        mx_next = store_last_scores(nxt) if nxt_last else store_scores(1 - slot, nxt)
        pv = _dot(vt_ref[0, jnp.maximum(j - 1, 0)], p_ref[1 - slot][...])

        def load(r0, nrows):
            sc = s_ref[slot][r0:r0 + nrows, :]
            return sc if diag is None else jnp.where(causal(diag, r0, nrows), sc, neg)

        if diag is not None:
            mx = load(0, SUBLANES)
            for r0 in range(SUBLANES, tk, SUBLANES):
                mx = jnp.maximum(mx, load(r0, SUBLANES))
        m_new = jnp.maximum(m, jnp.max(mx, axis=0, keepdims=True))
        alpha_new = jnp.exp2(m - m_new)
        m_rows = jnp.broadcast_to(m_new, (BF16_ROWS, tq))
        for r0 in range(0, tk, BF16_ROWS):
            p_ref[slot][r0:r0 + BF16_ROWS, :] = jnp.exp2(load(r0, BF16_ROWS) - m_rows).astype(BF16)
        acc_ref[...] = alpha * acc_ref[...] + pv
        return m_new, alpha_new, mx_next

    def two_stages(i, carry):
        carry = stage(2 * i, 0, *carry, nxt=2 * i + 1)
        return stage(2 * i + 1, 1, *carry, nxt=2 * i + 2)

    acc_ref[...] = jnp.zeros_like(acc_ref)
    p1_ref[...] = jnp.zeros_like(p1_ref)
    @pl.when(qi == 0)
    def _():
        mx0_ref[...] = store_scores(0, 0)

    mx0 = mx0_ref[...]
    carry = (jnp.full((1, tq), neg, F32), jnp.ones((1, tq), F32), mx0)
    carry = lax.fori_loop(0, qi, two_stages, carry)
    m, alpha, mx_last = stage(first_diag, 0, *carry, nxt=first_diag + 1, diag=0, nxt_last=True)
    acc = alpha * acc_ref[...] + _dot(vt_ref[0, first_diag], p0_ref[...])
    m_half = m[:, tk:]
    m_new = jnp.maximum(m_half, jnp.max(mx_last, axis=0, keepdims=True))
    m_rows = jnp.broadcast_to(m_new, (BF16_ROWS, tk))
    for r0 in range(0, tk, BF16_ROWS):
        p1_ref[r0:r0 + BF16_ROWS, tk:] = jnp.exp2(s1_ref[r0:r0 + BF16_ROWS, tk:] - m_rows).astype(BF16)
    right = jnp.exp2(m_half - m_new) * acc[:, tk:] + _dot(vt_ref[0, first_diag + 1], p1_ref[:, tk:])
    acc = jnp.concatenate([acc[:, :tk], right], axis=1)
    q_next = jnp.concatenate([qtn_ref[0, 0], qtn_ref[0, 1]], axis=1)
    sc_next = _dot(jnp.concatenate([kn_ref[0:tk, :], kr_ref[0:tk, :]], axis=1), q_next)
    s0_ref[...] = sc_next
    mx0_ref[...] = jnp.max(sc_next.reshape(tk // SUBLANES, SUBLANES, tq), axis=0)
    o = jnp.transpose(acc[:MLA_V, :] / acc[MLA_V:MLA_V + 1, :])
    o_ref[...] = (o * _silu(z_ref[...].astype(F32))).astype(o_ref.dtype)


def _flash(qt, kn, kr, vt, proj, off, b, s, tq, tk):
    t = kn.shape[0]
    nq = s // tq
    nk = s // tk
    cz = off["z_b"] // LANES
    return pl.pallas_call(
        functools.partial(_flash_kernel, tk=tk),
        grid=(b, MLA_HEADS, nq),
        in_specs=[
            pl.BlockSpec((1, tq // tk, 2 * LANES, tk), lambda bi, h, i: (h, bi * nq + i, 0, 0)),
            pl.BlockSpec((1, tq // tk, 2 * LANES, tk),
                         lambda bi, h, i: (h, jnp.minimum(bi * nq + i + 1, b * nq - 1), 0, 0)),
            pl.BlockSpec((s, LANES), lambda bi, h, i: (bi, h)),
            pl.BlockSpec((s, LANES), lambda bi, h, i: (bi, 0), pipeline_mode=pl.Buffered(1)),
            pl.BlockSpec((1, nk, VT_ROWS, tk), lambda bi, h, i: (h, bi, 0, 0)),
            pl.BlockSpec((tq, LANES), lambda bi, h, i: (bi * nq + i, cz + h)),
        ],
        out_specs=pl.BlockSpec((tq, LANES), lambda bi, h, i: (bi * nq + i, h)),
        out_shape=jax.ShapeDtypeStruct((t, MLA_HEADS * MLA_V), BF16),
        scratch_shapes=[
            pltpu.VMEM((tk, tq), F32), pltpu.VMEM((tk, tq), F32),
            pltpu.VMEM((tk, tq), BF16), pltpu.VMEM((tk, tq), BF16),
            pltpu.VMEM((VT_ROWS, tq), F32),
            pltpu.VMEM((SUBLANES, tq), F32),
        ],
        compiler_params=_cparams(("arbitrary", "arbitrary", "arbitrary")),
    )(qt, qt, kn, kr, vt, proj)


def _out_kernel(oa_ref, ob_ref, ga_ref, gb_ref, x_ref, pa_ref, pb_ref, wo_ref, fw_ref, y_ref):
    ya = _dot(oa_ref[...], pa_ref[...])
    yb = _dot(ob_ref[...], pb_ref[...])
    merged = _sigmoid(ga_ref[...].astype(F32)) * ya + _sigmoid(gb_ref[...].astype(F32)) * yb
    r = x_ref[...] + _dot(merged.astype(BF16), wo_ref[...])
    ms = jnp.mean(r * r, axis=-1, keepdims=True)
    y_ref[...] = r * lax.rsqrt(ms + NORM_EPS) * fw_ref[...]


def _out_stage(oa, ob, proj, x2, pa, pb, wo, fw, off, tm):
    t, d = x2.shape
    wv = oa.shape[1]
    const = lambda a: pl.BlockSpec(a.shape, lambda i: (0, 0), pipeline_mode=pl.Buffered(1))
    return pl.pallas_call(
        _out_kernel,
        grid=(t // tm,),
        in_specs=[
            pl.BlockSpec((tm, wv), lambda i: (i, 0)),
            pl.BlockSpec((tm, wv), lambda i: (i, 0)),
            pl.BlockSpec((tm, d), lambda i: (i, off["gate_a"] // d)),
            pl.BlockSpec((tm, d), lambda i: (i, off["gate_b"] // d)),
            pl.BlockSpec((tm, d), lambda i: (i, 0)),
            const(pa), const(pb), const(wo), const(fw),
        ],
        out_specs=pl.BlockSpec((tm, d), lambda i: (i, 0)),
        out_shape=jax.ShapeDtypeStruct((t, d), F32),
        compiler_params=_cparams(("parallel",)),
    )(oa, ob, proj, proj, x2, pa, pb, wo, fw)


def _layer(x2, pos_col, b, s, norm_w, w_in, conv_w, a_log, dt_bias, gdn_norm_w, q_norm_w, w_uq,
           kv_norm_w, w_ukv, proj_a, proj_b, w_out, out_norm_w):
    d = x2.shape[1]
    kw = GDN_HEADS * GDN_DK
    vw = GDN_HEADS * GDN_DV
    q_rank, kv_rank = w_uq.shape[0], w_ukv.shape[0]
    mv = MLA_HEADS * MLA_V
    widths = (2 * kw + vw, vw, GDN_HEADS, GDN_HEADS, q_rank, kv_rank, MLA_ROPE, mv, d, d)
    starts = [0]
    for wd in widths:
        starts.append(starts[-1] + wd)
    sl = lambda i: w_in[:, starts[i]:starts[i + 1]]
    w_qkv, w_za, w_alpha, w_beta, w_cq, w_ckv, w_kr, w_zb, w_ga, w_gb = (sl(i) for i in range(10))
    half = MLA_ROPE // 2
    w_kr1, w_kr2 = w_kr[:, :half], w_kr[:, half:]
    small_a = jnp.concatenate([w_kr1, w_kr2, w_kr2, w_kr1], axis=1)
    small_b = jnp.concatenate(
        [w_alpha, w_beta, jnp.zeros((d, LANES - 2 * GDN_HEADS), w_in.dtype)], axis=1)
    wide_groups = (("gate_a", w_ga), ("gate_b", w_gb), ("qkv", w_qkv), ("z_a", w_za), ("z_b", w_zb))
    narrow_groups = (("c_q", w_cq), ("c_kv", w_ckv), ("small_a", small_a), ("small_b", small_b))
    off = {}
    for groups in (wide_groups, narrow_groups):
        acc = 0
        for name, wg in groups:
            off[name] = acc
            acc += wg.shape[1]
    tn1 = acc
    wide_cols = sum(wg.shape[1] for _, wg in wide_groups)
    assert wide_cols % tn1 == 0
    w_all = jnp.concatenate([wg for _, wg in wide_groups + narrow_groups], axis=1).astype(BF16)

    t = x2.shape[0]
    tiles = _tile_sizes(t, s)
    wide, narrow = _inproj(x2, norm_w.reshape(1, d), w_all, tiles["inproj_rows"], tn1, wide_cols // tn1)

    pad = lambda a: jnp.pad(a.reshape(1, -1), ((0, 0), (0, LANES - a.shape[-1])))
    dt_pad = pad(dt_bias)
    qn, kn, vn, gb = _gdn_prep(wide, narrow, conv_w, pad(a_log), dt_pad, off, s, tiles["prep_rows"])
    oa = _gdn_scan(qn, kn, vn, gb, wide, gdn_norm_w.reshape(1, GDN_DV), off, b, s, tiles["scan_chunks"])

    wq3 = w_uq.reshape(q_rank, MLA_HEADS, MLA_QK)
    wq_nope = wq3[:, :, :MLA_NOPE].reshape(q_rank, MLA_HEADS * MLA_NOPE)
    r1, r2 = wq3[:, :, MLA_NOPE:MLA_NOPE + half], wq3[:, :, MLA_NOPE + half:]
    wq_rope = jnp.concatenate([r1, r2, r2, r1], axis=2).reshape(q_rank, MLA_HEADS * LANES)
    wq = jnp.concatenate([wq_nope, wq_rope], axis=1).astype(BF16)
    wkv3 = w_ukv.reshape(kv_rank, MLA_HEADS, MLA_NOPE + MLA_V)
    wkv = jnp.concatenate([wkv3[:, :, :MLA_NOPE].reshape(kv_rank, -1),
                           wkv3[:, :, MLA_NOPE:].reshape(kv_rank, -1)], axis=1).astype(BF16)
    inv_freq = ROPE_THETA ** (-jnp.arange(0, MLA_ROPE, 2, dtype=F32) / MLA_ROPE)
    invf = jnp.tile(inv_freq, LANES // half).reshape(1, LANES)
    qscale = math.log2(math.e) / math.sqrt(MLA_QK)
    tq, tk = tiles["flash_q"], tiles["flash_k"]
    qt, knope, krope, vt = _mla_prep(
        narrow, pos_col, q_norm_w.reshape(1, -1), kv_norm_w.reshape(1, -1), wq, wkv, invf, off, tk, qscale)
    ob = _flash(qt, knope, krope, vt, wide, off, b, s, tq, tk)

    return _out_stage(oa, ob, wide, x2, proj_a.astype(BF16), proj_b.astype(BF16), w_out.astype(BF16),
                      out_norm_w.reshape(1, d), off, tiles["out_rows"])


def kernel(x, positions, norm_w, w_in, conv_w, a_log, dt_bias, gdn_norm_w, q_norm_w, w_uq, kv_norm_w,
           w_ukv, proj_a, proj_b, w_out, final_norm_w):
    b, s, d = x.shape
    depth = norm_w.shape[0]
    assert depth == 1, "the final norm is fused into the last layer's output stage"
    x2 = x.reshape(b * s, d)
    pos_col = positions.astype(F32).reshape(b * s, 1)
    first = lambda a: a.reshape(a.shape[1:])
    y = _layer(x2, pos_col, b, s, first(norm_w), first(w_in), first(conv_w), first(a_log), first(dt_bias),
               first(gdn_norm_w), first(q_norm_w), first(w_uq), first(kv_norm_w), first(w_ukv), first(proj_a),
               first(proj_b), first(w_out), final_norm_w)
    return y.reshape(b, s, d)
```
